```python
import jax
import jax.numpy as jnp
from jax import lax
import numpy as np

D_MODEL = 1024
BATCH = 4
SEQ = 4096
DEPTH = 4
DEC_BATCH = 128
DEC_SEQ = 8
PAST_LEN = 2048
PAGE_SIZE = 128

N_MIXERS = 3
RET_HEADS = 4
RET_DK = D_MODEL // RET_HEADS
RET_DV = 2 * RET_DK
RET_CHUNK = 128
ROPE_BASE = 10000.0
MOBA_HEADS = 16
MOBA_HD = D_MODEL // MOBA_HEADS
MOBA_BLOCK = 256
MOBA_TOPK = 3
MOBA_Q_CHUNK = 64
CONV_WIDTH = 3
D_FF = 4 * D_MODEL
EPS = 1e-6
N_RET = len(range(0, DEPTH, N_MIXERS))
N_MOBA = len(range(1, DEPTH, N_MIXERS))
N_CONV = len(range(2, DEPTH, N_MIXERS))

kernel_name = "hybrid_retention_moba_shortconv_step"


def rmsnorm(x, g):
    xf = x.astype(jnp.float32)
    y = xf * lax.rsqrt(jnp.mean(xf * xf, axis=-1, keepdims=True) + EPS)
    return (y * g.astype(jnp.float32)).astype(x.dtype)


def rotary(x, pos):
    d = x.shape[-1]
    inv = ROPE_BASE ** (-jnp.arange(0, d, 2, dtype=jnp.float32) / d)
    ang = pos.astype(jnp.float32)[:, None] * inv[None, :]
    cos = jnp.cos(ang)[None, :, None, :]
    sin = jnp.sin(ang)[None, :, None, :]
    xf = x.astype(jnp.float32)
    x1 = xf[..., 0::2]
    x2 = xf[..., 1::2]
    out = jnp.stack([x1 * cos - x2 * sin, x1 * sin + x2 * cos], axis=-1)
    return out.reshape(x.shape)


def retention_chunk(state, q, k, v, log_g):
    c = q.shape[1]
    i = jnp.arange(c, dtype=jnp.float32)
    diff = i[:, None] - i[None, :]
    decay = jnp.where(diff >= 0, jnp.exp(log_g[:, None, None] * jnp.maximum(diff, 0.0)), 0.0)
    scores = jnp.einsum('bihd,bjhd->bhij', q, k) * decay[None]
    inner = jnp.einsum('bhij,bjhv->bihv', scores, v)
    q_decay = jnp.exp(log_g[None, :] * (i[:, None] + 1.0))
    cross = jnp.einsum('bihd,bhdv->bihv', q, state) * q_decay[None, :, :, None]
    k_decay = jnp.exp(log_g[None, :] * (c - 1.0 - i[:, None]))
    kv = jnp.einsum('bjhd,bjhv->bhdv', k * k_decay[None, :, :, None], v)
    new_state = jnp.exp(log_g * c)[None, :, None, None] * state + kv
    return new_state, inner + cross


def retention_mixer(h, state0, pos, w_qkvg, w_o, chunk):
    b, s, _ = h.shape
    hk = RET_HEADS * RET_DK
    hv = RET_HEADS * RET_DV
    proj = h @ w_qkvg
    q = rotary(proj[..., :hk].reshape(b, s, RET_HEADS, RET_DK), pos)
    k = rotary(proj[..., hk:2 * hk].reshape(b, s, RET_HEADS, RET_DK), pos) * (RET_DK ** -0.5)
    v = proj[..., 2 * hk:2 * hk + hv].reshape(b, s, RET_HEADS, RET_DV).astype(jnp.float32)
    g = proj[..., 2 * hk + hv:]
    log_g = jnp.log(1.0 - 2.0 ** (-5.0 - jnp.arange(RET_HEADS, dtype=jnp.float32)))
    nc = s // chunk

    def to_chunks(t):
        return t.reshape(b, nc, chunk, *t.shape[2:]).swapaxes(0, 1)

    def step(st, inp):
        qc, kc, vc = inp
        return retention_chunk(st, qc, kc, vc, log_g)

    final, o = lax.scan(step, state0.astype(jnp.float32), (to_chunks(q), to_chunks(k), to_chunks(v)))
    o = o.swapaxes(0, 1).reshape(b, s, RET_HEADS, RET_DV)
    o = o * lax.rsqrt(jnp.mean(o * o, axis=-1, keepdims=True) + EPS)
    o = o.reshape(b, s, hv).astype(h.dtype) * jax.nn.silu(g)
    return o @ w_o, final.astype(h.dtype)


def moba_attend(q, k_all, v_all, q_pos, q_chunk):
    b, sq, h, d = q.shape
    length = k_all.shape[1]
    nb = -(-length // MOBA_BLOCK)
    pad = nb * MOBA_BLOCK - length
    kb = jnp.pad(k_all, ((0, 0), (0, pad), (0, 0), (0, 0))).reshape(b, nb, MOBA_BLOCK, h, d)
    vb = jnp.pad(v_all, ((0, 0), (0, pad), (0, 0), (0, 0))).reshape(b, nb, MOBA_BLOCK, h, d)
    k_mean = jnp.mean(kb.astype(jnp.float32), axis=2)
    topk = min(MOBA_TOPK, nb)
    scale = d ** -0.5
    bi = jnp.arange(b)[:, None, None, None]
    hi = jnp.arange(h)[None, None, :, None]
    offs = jnp.arange(MOBA_BLOCK, dtype=jnp.int32)

    def one_chunk(args):
        qc, pc = args
        qcf = qc.astype(jnp.float32)
        own = pc // MOBA_BLOCK
        gate = jnp.einsum('bqhd,bnhd->bqhn', qcf, k_mean)
        fully_past = jnp.arange(nb)[None, :] < own[:, None]
        gate = jnp.where(fully_past[None, :, None, :], gate, -jnp.inf)
        _, top_idx = lax.top_k(gate, topk)
        own_idx = jnp.broadcast_to(own[None, :, None, None], (b, qc.shape[1], h, 1)).astype(top_idx.dtype)
        idx = jnp.concatenate([top_idx, own_idx], axis=-1)
        blk_ok = jnp.concatenate([top_idx < own[None, :, None, None], jnp.ones(own_idx.shape, dtype=bool)], axis=-1)
        k_sel = kb[bi, idx, :, hi]
        v_sel = vb[bi, idx, :, hi]
        k_pos = idx[..., None] * MOBA_BLOCK + offs
        mask = blk_ok[..., None] & (k_pos <= pc[None, :, None, None, None])
        sc = jnp.einsum('bqhd,bqhjkd->bqhjk', qc, k_sel).astype(jnp.float32) * scale
        sc = jnp.where(mask, sc, -jnp.inf)
        p = jax.nn.softmax(sc.reshape(*sc.shape[:3], -1), axis=-1).reshape(sc.shape)
        return jnp.einsum('bqhjk,bqhjkd->bqhd', p.astype(v_sel.dtype), v_sel)

    nq = sq // q_chunk
    qs = q.reshape(b, nq, q_chunk, h, d).swapaxes(0, 1)
    ps = q_pos.reshape(nq, q_chunk)
    o = lax.map(one_chunk, (qs, ps))
    return o.swapaxes(0, 1).reshape(b, sq, h, d)


def moba_mixer(h, pos, w_qkv, w_o, past_k, past_v, q_chunk):
    b, s, _ = h.shape
    qkv = (h @ w_qkv).reshape(b, s, 3, MOBA_HEADS, MOBA_HD)
    q, k, v = qkv[:, :, 0], qkv[:, :, 1], qkv[:, :, 2]
    if past_k is None:
        k_all, v_all = k, v
    else:
        k_all = jnp.concatenate([past_k.astype(k.dtype), k], axis=1)
        v_all = jnp.concatenate([past_v.astype(v.dtype), v], axis=1)
    o = moba_attend(q, k_all, v_all, pos, q_chunk)
    return o.reshape(b, s, D_MODEL) @ w_o, k, v


def conv_mixer(h, buf, w_in, w_conv, w_out):
    s = h.shape[1]
    b_gate, c_gate, xin = jnp.split(h @ w_in, 3, axis=-1)
    u = c_gate * xin
    ext = jnp.concatenate([buf.astype(u.dtype), u], axis=1)
    y = w_conv[0] * ext[:, 0:s]
    for j in range(1, CONV_WIDTH):
        y = y + w_conv[j] * ext[:, j:j + s]
    return (b_gate * y) @ w_out, ext[:, s:]


def mlp(h, w_up, w_down):
    a = jax.nn.relu(h @ w_up)
    return (a * a) @ w_down


def setup_inputs(seed: int = 0) -> dict:
    key = jax.random.key(seed)
    ks = jax.random.split(key, 20)
    n_pages = PAST_LEN // PAGE_SIZE
    n_phys = (DEC_BATCH * n_pages * 5) // 4

    def nrm(k, shape, scale):
        return jax.random.normal(k, shape, jnp.float32) * scale

    x_prompt = nrm(ks[0], (BATCH, SEQ, D_MODEL), 1.0)
    x_sample = nrm(ks[1], (DEC_BATCH, DEC_SEQ, D_MODEL), 1.0)
    state_ret = nrm(ks[2], (N_RET, DEC_BATCH, RET_HEADS, RET_DK, RET_DV), 0.5)
    cache_k = nrm(ks[3], (N_MOBA, n_phys, PAGE_SIZE, MOBA_HEADS, MOBA_HD), 1.0)
    cache_v = nrm(ks[4], (N_MOBA, n_phys, PAGE_SIZE, MOBA_HEADS, MOBA_HD), 1.0)
    state_conv = nrm(ks[5], (N_CONV, DEC_BATCH, CONV_WIDTH - 1, D_MODEL), 1.0)
    page_table = jax.random.permutation(ks[6], n_phys)[:DEC_BATCH * n_pages].reshape(DEC_BATCH, n_pages).astype(jnp.int32)
    norm_mix = 1.0 + nrm(ks[7], (DEPTH, D_MODEL), 0.01)
    norm_ff = 1.0 + nrm(ks[8], (DEPTH, D_MODEL), 0.01)
    norm_final = 1.0 + nrm(ks[9], (D_MODEL,), 0.01)
    ret_cols = 2 * RET_HEADS * RET_DK + 2 * RET_HEADS * RET_DV
    ret_w_qkvg = nrm(ks[10], (N_RET, D_MODEL, ret_cols), D_MODEL ** -0.5)
    ret_w_o = nrm(ks[11], (N_RET, RET_HEADS * RET_DV, D_MODEL), 0.5 * (RET_HEADS * RET_DV) ** -0.5)
    moba_w_qkv = nrm(ks[12], (N_MOBA, D_MODEL, 3 * D_MODEL), D_MODEL ** -0.5)
    moba_w_o = nrm(ks[13], (N_MOBA, D_MODEL, D_MODEL), 0.5 * D_MODEL ** -0.5)
    conv_w_in = nrm(ks[14], (N_CONV, D_MODEL, 3 * D_MODEL), D_MODEL ** -0.5)
    conv_w = nrm(ks[15], (N_CONV, CONV_WIDTH, D_MODEL), CONV_WIDTH ** -0.5)
    conv_w_out = nrm(ks[16], (N_CONV, D_MODEL, D_MODEL), 0.5 * D_MODEL ** -0.5)
    mlp_w_up = nrm(ks[17], (DEPTH, D_MODEL, D_FF), D_MODEL ** -0.5)
    mlp_w_down = nrm(ks[18], (DEPTH, D_FF, D_MODEL), 0.5 * D_FF ** -0.5)
    return {"x_prompt": x_prompt, "x_sample": x_sample, "state_ret": state_ret,
            "cache_k": cache_k, "cache_v": cache_v, "state_conv": state_conv,
            "page_table": page_table, "norm_mix": norm_mix, "norm_ff": norm_ff,
            "norm_final": norm_final, "ret_w_qkvg": ret_w_qkvg, "ret_w_o": ret_w_o,
            "moba_w_qkv": moba_w_qkv, "moba_w_o": moba_w_o, "conv_w_in": conv_w_in,
            "conv_w": conv_w, "conv_w_out": conv_w_out, "mlp_w_up": mlp_w_up,
            "mlp_w_down": mlp_w_down}


def reference(x_prompt, x_sample, state_ret, cache_k, cache_v, state_conv, page_table,
              norm_mix, norm_ff, norm_final, ret_w_qkvg, ret_w_o, moba_w_qkv, moba_w_o,
              conv_w_in, conv_w, conv_w_out, mlp_w_up, mlp_w_down):
    b, s, _ = x_prompt.shape
    db, t, _ = x_sample.shape
    n_pages = page_table.shape[1]
    past_len = n_pages * cache_k.shape[2]
    pos_p = jnp.arange(s, dtype=jnp.int32)
    pos_s = past_len + jnp.arange(t, dtype=jnp.int32)
    xp, xs = x_prompt, x_sample
    ret_p, ret_s, kp_l, vp_l, ks_l, vs_l, conv_p, conv_s = [], [], [], [], [], [], [], []
    for i in range(DEPTH):
        kind = i % N_MIXERS
        j = i // N_MIXERS
        hp = rmsnorm(xp, norm_mix[i])
        hs = rmsnorm(xs, norm_mix[i])
        if kind == 0:
            zero_state = jnp.zeros((b, RET_HEADS, RET_DK, RET_DV), jnp.float32)
            op, st_p = retention_mixer(hp, zero_state, pos_p, ret_w_qkvg[j], ret_w_o[j], RET_CHUNK)
            os_, st_s = retention_mixer(hs, state_ret[j], pos_s, ret_w_qkvg[j], ret_w_o[j], t)
            ret_p.append(st_p)
            ret_s.append(st_s)
        elif kind == 1:
            past_k = cache_k[j, page_table].reshape(db, past_len, MOBA_HEADS, MOBA_HD)
            past_v = cache_v[j, page_table].reshape(db, past_len, MOBA_HEADS, MOBA_HD)
            op, k_p, v_p = moba_mixer(hp, pos_p, moba_w_qkv[j], moba_w_o[j], None, None, MOBA_Q_CHUNK)
            os_, k_s, v_s = moba_mixer(hs, pos_s, moba_w_qkv[j], moba_w_o[j], past_k, past_v, 1)
            kp_l.append(k_p)
            vp_l.append(v_p)
            ks_l.append(k_s)
            vs_l.append(v_s)
        else:
            zero_buf = jnp.zeros((b, CONV_WIDTH - 1, D_MODEL), xp.dtype)
            op, buf_p = conv_mixer(hp, zero_buf, conv_w_in[j], conv_w[j], conv_w_out[j])
            os_, buf_s = conv_mixer(hs, state_conv[j], conv_w_in[j], conv_w[j], conv_w_out[j])
            conv_p.append(buf_p)
            conv_s.append(buf_s)
        xp = xp + op
        xs = xs + os_
        xp = xp + mlp(rmsnorm(xp, norm_ff[i]), mlp_w_up[i], mlp_w_down[i])
        xs = xs + mlp(rmsnorm(xs, norm_ff[i]), mlp_w_up[i], mlp_w_down[i])
    y_prompt = rmsnorm(xp, norm_final)
    y_sample = rmsnorm(xs, norm_final)
    return (y_prompt, y_sample, jnp.stack(ret_p), jnp.stack(ret_s), jnp.stack(kp_l), jnp.stack(vp_l),
            jnp.stack(ks_l), jnp.stack(vs_l), jnp.stack(conv_p), jnp.stack(conv_s))
```

```python
import functools

import jax
import jax.numpy as jnp
from jax import lax
from jax.experimental import pallas as pl
from jax.experimental.pallas import tpu as pltpu

F32 = jnp.float32
BF16 = jnp.bfloat16
NEG_INF = float("-inf")

D_MODEL = 1024
N_MIXERS = 3
RET_HEADS = 4
RET_DK = D_MODEL // RET_HEADS
RET_DV = 2 * RET_DK
RET_HALF = RET_DK // 2
ROPE_BASE = 10000.0
MOBA_HEADS = 16
MOBA_HD = D_MODEL // MOBA_HEADS
MOBA_BLOCK = 256
MOBA_TOPK = 3
CONV_WIDTH = 3
D_FF = 4 * D_MODEL
EPS = 1e-6

LANES = 128
SUBLANES = 8
HEADS_PER_LANE_TILE = LANES // MOBA_HD
PAGES_PER_STEP = 4

_HI = lax.Precision.HIGHEST


def _params(*sem):
    return pltpu.CompilerParams(dimension_semantics=sem)


def _row_tile(m, want):
    t = min(m, want)
    assert m % t == 0, (m, t)
    return t


def _rms(x, gamma):
    ms = jnp.mean(x * x, axis=-1, keepdims=True)
    return x * lax.rsqrt(ms + EPS) * gamma


def _dot(a, b):
    return jnp.dot(a, b, preferred_element_type=F32)


def _dot_nt(a, b):
    return lax.dot_general(a, b, (((1,), (1,)), ((), ())), preferred_element_type=F32)


def _dot_tn(a, b):
    return lax.dot_general(a, b, (((0,), (0,)), ((), ())), preferred_element_type=F32)


def _norm_proj_kernel(x_ref, g_ref, w_ref, *out_refs, widths, mean_rows):
    h = _rms(x_ref[...], g_ref[...]).astype(BF16)
    n_main = len(widths)
    off = 0
    for idx, wd in enumerate(widths):
        p = _dot(h, w_ref[:, off:off + wd])
        out_refs[idx][...] = p.astype(out_refs[idx].dtype)
        if mean_rows and idx == mean_rows[0]:
            blk = mean_rows[1]
            km_ref = out_refs[n_main]
            for r in range(p.shape[0] // blk):
                km_ref[r] = jnp.mean(p[r * blk:(r + 1) * blk], axis=0, keepdims=True)
        off += wd


def _norm_proj(x, gamma, w, widths, dtypes, *, tm, mean_rows=None):
    m, d = x.shape
    tm = _row_tile(m, tm)
    n = sum(widths)
    out_shape = [jax.ShapeDtypeStruct((m, wd), dt) for wd, dt in zip(widths, dtypes)]
    out_specs = [pl.BlockSpec((tm, wd), lambda i: (i, 0)) for wd in widths]
    if mean_rows:
        blk = mean_rows[1]
        assert tm % blk == 0
        out_shape.append(jax.ShapeDtypeStruct((m // blk, 1, widths[mean_rows[0]]), F32))
        out_specs.append(pl.BlockSpec((tm // blk, 1, widths[mean_rows[0]]), lambda i: (i, 0, 0)))
    return pl.pallas_call(
        functools.partial(_norm_proj_kernel, widths=tuple(widths), mean_rows=mean_rows),
        grid=(m // tm,),
        in_specs=[pl.BlockSpec((tm, d), lambda i: (i, 0)),
                  pl.BlockSpec((1, d), lambda i: (0, 0)),
                  pl.BlockSpec((d, n), lambda i: (0, 0))],
        out_specs=out_specs,
        out_shape=out_shape,
        compiler_params=_params("arbitrary"),
        name="norm_proj",
    )(x, gamma, w)


def _ret_proj_kernel(x_ref, g_ref, w_ref, cos_ref, sin_ref, q_ref, k_ref, v_ref, gate_ref):
    h = _rms(x_ref[...], g_ref[...]).astype(BF16)
    cos = cos_ref[...]
    sin = sin_ref[...]
    hk = RET_HEADS * RET_DK
    hv = RET_HEADS * RET_DV
    for off, scale, out in ((0, 1.0, q_ref), (hk, RET_DK ** -0.5, k_ref)):
        p = _dot(h, w_ref[:, off:off + hk])
        for hd in range(RET_HEADS):
            lo = hd * RET_DK
            xe = p[:, lo:lo + RET_HALF]
            xo = p[:, lo + RET_HALF:lo + RET_DK]
            out[:, lo:lo + RET_HALF] = ((xe * cos - xo * sin) * scale).astype(out.dtype)
            out[:, lo + RET_HALF:lo + RET_DK] = ((xe * sin + xo * cos) * scale).astype(out.dtype)
    v_ref[...] = _dot(h, w_ref[:, 2 * hk:2 * hk + hv]).astype(v_ref.dtype)
    gate_ref[...] = _dot(h, w_ref[:, 2 * hk + hv:])


def _ret_proj(x, gamma, w, cos, sin, *, tm, qv_dtype):
    m, d = x.shape
    tm = _row_tile(m, tm)
    hk = RET_HEADS * RET_DK
    hv = RET_HEADS * RET_DV
    n_tab = cos.shape[0] // tm
    assert cos.shape[0] % tm == 0
    row = lambda i: (i, 0)
    return pl.pallas_call(
        _ret_proj_kernel,
        grid=(m // tm,),
        in_specs=[pl.BlockSpec((tm, d), row),
                  pl.BlockSpec((1, d), lambda i: (0, 0)),
                  pl.BlockSpec((d, 2 * hk + 2 * hv), lambda i: (0, 0)),
                  pl.BlockSpec((tm, RET_HALF), lambda i: (i % n_tab, 0)),
                  pl.BlockSpec((tm, RET_HALF), lambda i: (i % n_tab, 0))],
        out_specs=[pl.BlockSpec((tm, hk), row), pl.BlockSpec((tm, hk), row),
                   pl.BlockSpec((tm, hv), row), pl.BlockSpec((tm, hv), row)],
        out_shape=[jax.ShapeDtypeStruct((m, hk), qv_dtype), jax.ShapeDtypeStruct((m, hk), F32),
                   jax.ShapeDtypeStruct((m, hv), qv_dtype), jax.ShapeDtypeStruct((m, hv), F32)],
        compiler_params=_params("arbitrary"),
        name="ret_proj",
    )(x, gamma, w, cos, sin)


def _ret_core_kernel(*refs, nb, chunk, n_chunks, has_state):
    if has_state:
        q_ref, k_ref, v_ref, g_ref, dm_ref, qd_ref, kd_ref, st0_ref, o_ref, stout_ref, st_scr = refs
    else:
        q_ref, k_ref, v_ref, g_ref, dm_ref, qd_ref, kd_ref, o_ref, stout_ref, st_scr = refs
    c = pl.program_id(2)

    n_lt = RET_DV // LANES

    def hbm_rows(parity, t):
        return pl.ds(parity * n_lt + t, RET_HALF, stride=2 * n_lt)

    @pl.when(c == 0)
    def _():
        for s in range(nb):
            if has_state:
                for parity in range(2):
                    for t in range(n_lt):
                        st_scr[s, parity * RET_HALF:(parity + 1) * RET_HALF, t * LANES:(t + 1) * LANES] = (
                            st0_ref[s, 0, hbm_rows(parity, t), :])
            else:
                st_scr[s] = jnp.zeros((RET_DK, RET_DV), F32)

    dm = dm_ref[0]
    qd = qd_ref[0]
    kd = kd_ref[0]
    qd_v = jnp.concatenate([qd] * (RET_DV // LANES), axis=1)
    kd_k = jnp.concatenate([kd] * (RET_DK // LANES), axis=1)
    sdec = qd[chunk - 1:chunk, 0:1]
    for s in range(nb):
        q = q_ref[s].astype(BF16)
        kf = k_ref[s]
        v = v_ref[s].astype(BF16)
        st = st_scr[s]
        sc = _dot_nt(q, kf.astype(BF16)) * dm
        inner = _dot(sc.astype(BF16), v)
        cross = _dot(q, st.astype(BF16)) * qd_v
        o = inner + cross
        kv = _dot_tn((kf * kd_k).astype(BF16), v)
        st_new = sdec * st + kv
        st_scr[s] = st_new
        on = o * lax.rsqrt(jnp.mean(o * o, axis=-1, keepdims=True) + EPS)
        gg = g_ref[s]
        o_ref[s] = (on * (gg * jax.nn.sigmoid(gg))).astype(o_ref.dtype)

        @pl.when(c == n_chunks - 1)
        def _():
            for parity in range(2):
                for t in range(n_lt):
                    stout_ref[s, 0, hbm_rows(parity, t), :] = (
                        st_new[parity * RET_HALF:(parity + 1) * RET_HALF, t * LANES:(t + 1) * LANES])


def _ret_core(q, k, v, g, tabs, state0, *, batch, chunk, nb):
    hk = RET_HEADS * RET_DK
    hv = RET_HEADS * RET_DV
    s_len = q.shape[0] // batch
    n_chunks = s_len // chunk
    assert s_len % chunk == 0 and batch % nb == 0
    q3 = q.reshape(batch, s_len, hk)
    k3 = k.reshape(batch, s_len, hk)
    v3 = v.reshape(batch, s_len, hv)
    g3 = g.reshape(batch, s_len, hv)
    dm, qd, kd = tabs
    seq = lambda b, h, c: (b, c, h)
    tab = lambda b, h, c: (h, 0, 0)
    in_specs = [pl.BlockSpec((nb, chunk, RET_DK), seq), pl.BlockSpec((nb, chunk, RET_DK), seq),
                pl.BlockSpec((nb, chunk, RET_DV), seq), pl.BlockSpec((nb, chunk, RET_DV), seq),
                pl.BlockSpec((1, chunk, chunk), tab), pl.BlockSpec((1, chunk, LANES), tab),
                pl.BlockSpec((1, chunk, LANES), tab)]
    args = [q3, k3, v3, g3, dm, qd, kd]
    st_rows = RET_DK * RET_DV // LANES
    st_spec = pl.BlockSpec((nb, 1, st_rows, LANES), lambda b, h, c: (b, h, 0, 0))
    if state0 is not None:
        in_specs.append(st_spec)
        args.append(state0.reshape(batch, RET_HEADS, st_rows, LANES))
    o, st = pl.pallas_call(
        functools.partial(_ret_core_kernel, nb=nb, chunk=chunk, n_chunks=n_chunks,
                          has_state=state0 is not None),
        grid=(batch // nb, RET_HEADS, n_chunks),
        in_specs=in_specs,
        out_specs=[pl.BlockSpec((nb, chunk, RET_DV), seq), st_spec],
        out_shape=[jax.ShapeDtypeStruct((batch, s_len, hv), BF16),
                   jax.ShapeDtypeStruct((batch, RET_HEADS, st_rows, LANES), F32)],
        scratch_shapes=[pltpu.VMEM((nb, RET_DK, RET_DV), F32)],
        compiler_params=_params("arbitrary", "arbitrary", "arbitrary"),
        name="ret_core",
    )(*args)
    return o.reshape(batch * s_len, hv), st.reshape(batch, RET_HEADS, RET_DK, RET_DV)


def _ret_tables(chunk):
    log_g = jnp.log(1.0 - 2.0 ** (-5.0 - jnp.arange(RET_HEADS, dtype=F32)))
    i = jnp.arange(chunk, dtype=F32)
    diff = i[:, None] - i[None, :]
    dm = jnp.where(diff >= 0, jnp.exp(log_g[:, None, None] * jnp.maximum(diff, 0.0)), 0.0)
    qd = jnp.exp(log_g[:, None] * (i[None, :] + 1.0))
    kd = jnp.exp(log_g[:, None] * (chunk - 1.0 - i[None, :]))
    rep = lambda t: jnp.broadcast_to(t[:, :, None], (RET_HEADS, chunk, LANES))
    return dm, rep(qd), rep(kd)


def _rope_tables(pos):
    inv = ROPE_BASE ** (-jnp.arange(0, RET_DK, 2, dtype=F32) / RET_DK)
    ang = pos.astype(F32)[:, None] * inv[None, :]
    return jnp.cos(ang), jnp.sin(ang)


def _proj_res_kernel(a_ref, w_ref, x_ref, o_ref):
    o_ref[...] = x_ref[...] + _dot(a_ref[...].astype(BF16), w_ref[...])


def _proj_res(a, w, x, *, tm):
    m, kdim = a.shape
    d = x.shape[1]
    tm = _row_tile(m, tm)
    return pl.pallas_call(
        _proj_res_kernel,
        grid=(m // tm,),
        in_specs=[pl.BlockSpec((tm, kdim), lambda i: (i, 0)),
                  pl.BlockSpec((kdim, d), lambda i: (0, 0)),
                  pl.BlockSpec((tm, d), lambda i: (i, 0))],
        out_specs=pl.BlockSpec((tm, d), lambda i: (i, 0)),
        out_shape=jax.ShapeDtypeStruct((m, d), F32),
        compiler_params=_params("arbitrary"),
        name="proj_res",
    )(a, w, x)


def _mlp_kernel(x_ref, g_ref, wu_ref, wd_ref, o_ref, h_scr, acc_scr, *, n_f):
    f = pl.program_id(1)

    @pl.when(f == 0)
    def _():
        h_scr[...] = _rms(x_ref[...], g_ref[...]).astype(BF16)

    a = jnp.maximum(_dot(h_scr[...], wu_ref[...]), 0.0)
    part = _dot((a * a).astype(BF16), wd_ref[...])

    @pl.when(f == 0)
    def _():
        acc_scr[...] = part

    @pl.when(f > 0)
    def _():
        acc_scr[...] += part

    @pl.when(f == n_f - 1)
    def _():
        o_ref[...] = x_ref[...] + acc_scr[...]


def _mlp(x, gamma, w_up, w_down, *, tm, tf):
    m, d = x.shape
    tm = _row_tile(m, tm)
    n_f = D_FF // tf
    return pl.pallas_call(
        functools.partial(_mlp_kernel, n_f=n_f),
        grid=(m // tm, n_f),
        in_specs=[pl.BlockSpec((tm, d), lambda i, f: (i, 0)),
                  pl.BlockSpec((1, d), lambda i, f: (0, 0)),
                  pl.BlockSpec((d, tf), lambda i, f: (0, f)),
                  pl.BlockSpec((tf, d), lambda i, f: (f, 0))],
        out_specs=pl.BlockSpec((tm, d), lambda i, f: (i, 0)),
        out_shape=jax.ShapeDtypeStruct((m, d), F32),
        scratch_shapes=[pltpu.VMEM((tm, d), BF16), pltpu.VMEM((tm, d), F32)],
        compiler_params=_params("arbitrary", "arbitrary"),
        name="mlp",
    )(x, gamma, w_up, w_down)


def _conv_kernel(*refs, tiles_per_seq, seq_in_tile):
    if seq_in_tile:
        x_ref, g_ref, win_ref, wc_ref, wout_ref, p1_ref, p2_ref, o_ref, u_ref = refs
    else:
        x_ref, g_ref, win_ref, wc_ref, wout_ref, o_ref, u_ref, carry_scr = refs
    d = D_MODEL
    x = x_ref[...]
    tm = x.shape[0]
    h = _rms(x, g_ref[...]).astype(BF16)
    bg = _dot(h, win_ref[:, 0:d])
    u = _dot(h, win_ref[:, d:2 * d]) * _dot(h, win_ref[:, 2 * d:3 * d])
    row = lax.broadcasted_iota(jnp.int32, (tm, 1), 0)
    r1 = pltpu.roll(u, 1, 0)
    r2 = pltpu.roll(u, 2, 0)
    if seq_in_tile:
        t = row % seq_in_tile
        um1 = jnp.where(t >= 1, r1, 0.0) + p1_ref[...]
        um2 = jnp.where(t >= 2, r2, 0.0) + p2_ref[...]
        u_ref[...] = u
    else:
        @pl.when((pl.program_id(0) % tiles_per_seq) == 0)
        def _():
            carry_scr[...] = jnp.zeros((SUBLANES, d), F32)

        carry = carry_scr[...]
        c1 = carry[SUBLANES - 1:SUBLANES]
        c2 = carry[SUBLANES - 2:SUBLANES - 1]
        um1 = jnp.where(row >= 1, r1, c1)
        um2 = jnp.where(row >= 2, r2, jnp.where(row == 1, c1, c2))
        tail = u[tm - SUBLANES:tm]
        carry_scr[...] = tail
        u_ref[...] = tail
    wc = wc_ref[...]
    y = wc[0:1] * um2 + wc[1:2] * um1 + wc[2:3] * u
    o_ref[...] = x + _dot((bg * y).astype(BF16), wout_ref[...])


def _conv_mixer(x, gamma, w_in, w_c, w_out, *, tm, seq_len, state=None):
    m, d = x.shape
    n_seq = m // seq_len
    wc_pad = jnp.zeros((SUBLANES, d), F32).at[:CONV_WIDTH].set(w_c)
    row = lambda i: (i, 0)
    const = lambda i: (0, 0)
    in_specs = [pl.BlockSpec((tm, d), row), pl.BlockSpec((1, d), const),
                pl.BlockSpec((d, 3 * d), const), pl.BlockSpec((SUBLANES, d), const),
                pl.BlockSpec((d, d), const)]
    args = [x, gamma, w_in, wc_pad, w_out]
    if state is not None:
        tm = _row_tile(m, tm)
        assert tm % seq_len == 0 and seq_len >= CONV_WIDTH - 1
        zeros = jnp.zeros((n_seq, seq_len, d), F32)
        p1 = zeros.at[:, 0].set(state[:, 1]).reshape(m, d)
        p2 = zeros.at[:, 0].set(state[:, 0]).at[:, 1].set(state[:, 1]).reshape(m, d)
        in_specs += [pl.BlockSpec((tm, d), row), pl.BlockSpec((tm, d), row)]
        args += [p1, p2]
        u_rows, u_blk, scratch = m, tm, []
        kern = functools.partial(_conv_kernel, tiles_per_seq=0, seq_in_tile=seq_len)
    else:
        tm = _row_tile(seq_len, tm)
        u_rows, u_blk = (m // tm) * SUBLANES, SUBLANES
        scratch = [pltpu.VMEM((SUBLANES, d), F32)]
        kern = functools.partial(_conv_kernel, tiles_per_seq=seq_len // tm, seq_in_tile=0)
    out, u = pl.pallas_call(
        kern,
        grid=(m // tm,),
        in_specs=in_specs,
        out_specs=[pl.BlockSpec((tm, d), row), pl.BlockSpec((u_blk, d), row)],
        out_shape=[jax.ShapeDtypeStruct((m, d), F32), jax.ShapeDtypeStruct((u_rows, d), F32)],
        scratch_shapes=scratch,
        compiler_params=_params("arbitrary"),
        name="conv_mixer",
    )(*args)
    u_last = u.reshape(n_seq, -1, d)[:, -(CONV_WIDTH - 1):]
    return out, u_last


def _topk_bits(gate, blk, n_valid):
    nblk = gate.shape[1]
    g = jnp.where(blk < n_valid, gate, NEG_INF)
    bits = jnp.zeros((gate.shape[0], 1), jnp.int32)
    for _ in range(MOBA_TOPK):
        mx = jnp.max(g, axis=1, keepdims=True)
        idx = jnp.min(jnp.where(g == mx, blk, nblk), axis=1, keepdims=True)
        bits = bits | jnp.where(idx < n_valid, jnp.left_shift(1, idx), 0)
        g = jnp.where(blk == idx, NEG_INF, g)
    return bits


def _moba_prompt_kernel(q_ref, k_ref, v_ref, km_ref, o_ref, *, nblk):
    i = pl.program_id(2)
    bs = MOBA_BLOCK
    qf = q_ref[...]
    km = km_ref[...]
    lane = lax.broadcasted_iota(jnp.int32, (1, LANES), 1)
    blk = lax.broadcasted_iota(jnp.int32, (1, nblk), 1)
    causal = (lax.broadcasted_iota(jnp.int32, (bs, 1), 0)
              >= lax.broadcasted_iota(jnp.int32, (1, bs), 1))
    own = pl.ds(pl.multiple_of(i * bs, bs), bs)
    k_own = k_ref[own, :].astype(BF16)
    v_own = v_ref[own, :].astype(BF16)

    head_masks = [lane < MOBA_HD, lane >= MOBA_HD]
    qbs, bits, carry = [], [], []
    for hm in head_masks:
        qh = jnp.where(hm, qf, 0.0)
        gate = lax.dot_general(qh, km, (((1,), (1,)), ((), ())), precision=_HI,
                               preferred_element_type=F32)
        bits.append(_topk_bits(gate, blk, i))
        qb = (qh * (MOBA_HD ** -0.5)).astype(BF16)
        qbs.append(qb)
        s = jnp.where(causal, _dot_nt(qb, k_own), NEG_INF)
        m = jnp.max(s, axis=1, keepdims=True)
        p = jnp.exp(s - m)
        carry += [m, jnp.sum(p, axis=1, keepdims=True), _dot(p.astype(BF16), v_own)]

    def body(j, carry):
        rows = pl.ds(pl.multiple_of(j * bs, bs), bs)
        kj = k_ref[rows, :].astype(BF16)
        vj = v_ref[rows, :].astype(BF16)
        new = []
        for h2 in range(2):
            m, l, acc = carry[3 * h2:3 * h2 + 3]
            picked = (jnp.right_shift(bits[h2], j) & 1) == 1
            s = jnp.where(picked, _dot_nt(qbs[h2], kj), NEG_INF)
            m_new = jnp.maximum(m, jnp.max(s, axis=1, keepdims=True))
            alpha = jnp.exp(m - m_new)
            p = jnp.exp(s - m_new)
            new += [m_new, alpha * l + jnp.sum(p, axis=1, keepdims=True),
                    alpha * acc + _dot(p.astype(BF16), vj)]
        return tuple(new)

    carry = lax.fori_loop(0, i, body, tuple(carry))
    o0 = carry[2] / carry[1]
    o1 = carry[5] / carry[4]
    o_ref[...] = jnp.where(head_masks[0], o0, o1).astype(o_ref.dtype)


def _moba_prompt(q, k, v, kmean, *, batch):
    m, d = q.shape
    s_len = m // batch
    nblk = s_len // MOBA_BLOCK
    assert s_len % MOBA_BLOCK == 0 and nblk % SUBLANES == 0
    n_hp = d // LANES
    km2 = kmean.reshape(batch * nblk, d)
    return pl.pallas_call(
        functools.partial(_moba_prompt_kernel, nblk=nblk),
        grid=(batch, n_hp, nblk),
        in_specs=[pl.BlockSpec((MOBA_BLOCK, LANES), lambda b, h, i: (b * nblk + i, h)),
                  pl.BlockSpec((s_len, LANES), lambda b, h, i: (b, h)),
                  pl.BlockSpec((s_len, LANES), lambda b, h, i: (b, h)),
                  pl.BlockSpec((nblk, LANES), lambda b, h, i: (b, h))],
        out_specs=pl.BlockSpec((MOBA_BLOCK, LANES), lambda b, h, i: (b * nblk + i, h)),
        out_shape=jax.ShapeDtypeStruct((m, d), BF16),
        compiler_params=_params("arbitrary", "arbitrary", "arbitrary"),
        name="moba_prompt",
    )(q, k, v, km2)


def _moba_paged_kernel(pt_ref, q_ref, kn_ref, vn_ref, *refs, n_pages, page, t_new):
    pg = PAGES_PER_STEP
    k_refs = refs[0:pg]
    v_refs = refs[pg:2 * pg]
    o_ref = refs[2 * pg]
    qbd_scr, qbdf_scr, s_scr, p_scr, ksum_scr, pown_scr, acc_scr = refs[2 * pg + 1:]
    del pt_ref
    ph = pl.program_id(1)
    n = pl.program_id(2)
    n_grp = n_pages // pg
    n_blk = n_pages * page // MOBA_BLOCK
    pages_per_blk = MOBA_BLOCK // page
    n_col = MOBA_HEADS * t_new
    d = D_MODEL

    @pl.when((ph == 0) & (n == 0))
    def _():
        q = q_ref[0]
        sel_t = (lax.broadcasted_iota(jnp.int32, (t_new, n_col), 0)
                 == lax.broadcasted_iota(jnp.int32, (t_new, n_col), 1) % t_new).astype(F32)
        qt = lax.dot_general(q, sel_t, (((0,), (0,)), ((), ())), precision=_HI,
                             preferred_element_type=F32)
        bd = (lax.broadcasted_iota(jnp.int32, (d, n_col), 0) // MOBA_HD
              == lax.broadcasted_iota(jnp.int32, (d, n_col), 1) // t_new)
        qbdf = jnp.where(bd, qt, 0.0)
        qbdf_scr[...] = qbdf
        qbd_scr[...] = (qbdf * (MOBA_HD ** -0.5)).astype(BF16)

    @pl.when(ph == 0)
    def _():
        qbd = qbd_scr[...]
        sums = []
        for r in range(pg):
            kp = k_refs[r][0]
            rows = pl.ds(pl.multiple_of((n * pg + r) * page, page), page)
            s_scr[rows, :] = _dot(kp.astype(BF16), qbd)
            sums.append(jnp.sum(kp, axis=0, keepdims=True))
        for b2 in range(pg // pages_per_blk):
            tot = sums[b2 * pages_per_blk]
            for r in range(1, pages_per_blk):
                tot = tot + sums[b2 * pages_per_blk + r]
            ksum_scr[pl.ds(n * (pg // pages_per_blk) + b2, 1), :] = tot

    @pl.when((ph == 1) & (n == 0))
    def _():
        kmean = ksum_scr[...] * (1.0 / MOBA_BLOCK)
        gate = lax.dot_general(kmean, qbdf_scr[...], (((1,), (0,)), ((), ())), precision=_HI,
                               preferred_element_type=F32)
        blk = lax.broadcasted_iota(jnp.int32, (n_blk, n_col), 0)
        g = gate
        sel = jnp.zeros((n_blk, n_col), jnp.bool_)
        for _ in range(MOBA_TOPK):
            mx = jnp.max(g, axis=0, keepdims=True)
            idx = jnp.min(jnp.where(g == mx, blk, n_blk), axis=0, keepdims=True)
            pick = blk == idx
            sel = sel | pick
            g = jnp.where(pick, NEG_INF, g)
        s_own = _dot(kn_ref[0].astype(BF16), qbd_scr[...])
        ok_own = (lax.broadcasted_iota(jnp.int32, (t_new, n_col), 0)
                  <= lax.broadcasted_iota(jnp.int32, (t_new, n_col), 1) % t_new)
        s_own = jnp.where(ok_own, s_own, NEG_INF)
        m = jnp.max(s_own, axis=0, keepdims=True)
        masked = []
        for b in range(n_blk):
            sb = jnp.where(sel[b:b + 1], s_scr[b * MOBA_BLOCK:(b + 1) * MOBA_BLOCK, :], NEG_INF)
            masked.append(sb)
            m = jnp.maximum(m, jnp.max(sb, axis=0, keepdims=True))
        e_own = jnp.exp(s_own - m)
        l = jnp.sum(e_own, axis=0, keepdims=True)
        es = []
        for sb in masked:
            e = jnp.exp(sb - m)
            es.append(e)
            l = l + jnp.sum(e, axis=0, keepdims=True)
        inv = 1.0 / l
        for b, e in enumerate(es):
            p_scr[b * MOBA_BLOCK:(b + 1) * MOBA_BLOCK, :] = (e * inv).astype(BF16)
        p_own = (e_own * inv).astype(BF16)
        acc_scr[...] = _dot_tn(p_own, vn_ref[0].astype(BF16))

    @pl.when(ph == 1)
    def _():
        acc = acc_scr[...]
        for r in range(pg):
            rows = pl.ds(pl.multiple_of((n * pg + r) * page, page), page)
            acc = acc + _dot_tn(p_scr[rows, :], v_refs[r][0].astype(BF16))
        acc_scr[...] = acc

        @pl.when(n == n_grp - 1)
        def _():
            bd = (lax.broadcasted_iota(jnp.int32, (n_col, d), 0) // t_new
                  == lax.broadcasted_iota(jnp.int32, (n_col, d), 1) // MOBA_HD)
            a = jnp.where(bd, acc, 0.0)
            out = a[0:t_new]
            for hd in range(1, MOBA_HEADS):
                out = out + a[hd * t_new:(hd + 1) * t_new]
            o_ref[0] = out


def _moba_paged(q, k_new, v_new, cache_k, cache_v, page_table):
    db, t_new, d = q.shape
    n_pages = page_table.shape[1]
    page = cache_k.shape[1]
    pg = PAGES_PER_STEP
    past = n_pages * page
    assert past % MOBA_BLOCK == 0 and MOBA_BLOCK % page == 0 and n_pages % pg == 0
    assert pg % (MOBA_BLOCK // page) == 0 and t_new <= MOBA_BLOCK
    assert past // MOBA_BLOCK >= MOBA_TOPK and MOBA_HEADS * t_new == LANES
    n_grp = n_pages // pg
    n_blk = past // MOBA_BLOCK
    n_col = MOBA_HEADS * t_new
    pt = page_table.reshape(-1).astype(jnp.int32)

    def k_map(r):
        return lambda b, ph, n, pt_ref: (
            pt_ref[b * n_pages + jnp.where(ph == 0, n, n_grp - 1) * pg + r], 0, 0)

    def v_map(r):
        return lambda b, ph, n, pt_ref: (
            pt_ref[b * n_pages + jnp.where(ph == 0, 0, n) * pg + r], 0, 0)

    tok = pl.BlockSpec((1, t_new, d), lambda b, ph, n, pt_ref: (b, 0, 0))
    in_specs = [tok, tok, tok]
    in_specs += [pl.BlockSpec((1, page, d), k_map(r)) for r in range(pg)]
    in_specs += [pl.BlockSpec((1, page, d), v_map(r)) for r in range(pg)]
    grid_spec = pltpu.PrefetchScalarGridSpec(
        num_scalar_prefetch=1,
        grid=(db, 2, n_grp),
        in_specs=in_specs,
        out_specs=pl.BlockSpec((1, t_new, d), lambda b, ph, n, pt_ref: (b, 0, 0)),
        scratch_shapes=[pltpu.VMEM((d, n_col), BF16), pltpu.VMEM((d, n_col), F32),
                        pltpu.VMEM((past, n_col), F32), pltpu.VMEM((past, n_col), BF16),
                        pltpu.VMEM((n_blk, d), F32), pltpu.VMEM((t_new, n_col), BF16),
                        pltpu.VMEM((n_col, d), F32)],
    )
    return pl.pallas_call(
        functools.partial(_moba_paged_kernel, n_pages=n_pages, page=page, t_new=t_new),
        grid_spec=grid_spec,
        out_shape=jax.ShapeDtypeStruct((db, t_new, d), F32),
        compiler_params=_params("arbitrary", "arbitrary", "arbitrary"),
        name="moba_paged",
    )(pt, q, k_new, v_new, *([cache_k] * pg), *([cache_v] * pg))


def _final_norm_kernel(x_ref, g_ref, o_ref):
    o_ref[...] = _rms(x_ref[...], g_ref[...])


def _final_norm(x, gamma, *, tm):
    m, d = x.shape
    tm = _row_tile(m, tm)
    return pl.pallas_call(
        _final_norm_kernel,
        grid=(m // tm,),
        in_specs=[pl.BlockSpec((tm, d), lambda i: (i, 0)), pl.BlockSpec((1, d), lambda i: (0, 0))],
        out_specs=pl.BlockSpec((tm, d), lambda i: (i, 0)),
        out_shape=jax.ShapeDtypeStruct((m, d), F32),
        compiler_params=_params("arbitrary"),
        name="final_norm",
    )(x, gamma)


def _half_split_perm():
    d = jnp.arange(RET_DK)
    per_head = jnp.concatenate([d[0::2], d[1::2]])
    return (jnp.arange(RET_HEADS)[:, None] * RET_DK + per_head[None, :]).reshape(-1)


def kernel(x_prompt, x_sample, state_ret, cache_k, cache_v, state_conv, page_table, norm_mix, norm_ff, norm_final, ret_w_qkvg, ret_w_o, moba_w_qkv, moba_w_o, conv_w_in, conv_w, conv_w_out, mlp_w_up, mlp_w_down):
    b, s, d = x_prompt.shape
    db, t, _ = x_sample.shape
    depth = norm_mix.shape[0]
    n_pages = page_table.shape[1]
    page = cache_k.shape[2]
    past_len = n_pages * page
    hk = RET_HEADS * RET_DK

    xp = x_prompt.reshape(b * s, d)
    xs = x_sample.reshape(db * t, d)
    tm_p = 512
    tm_s = db * t

    ret_chunk = 256
    ret_nb_s = 8
    cos_p, sin_p = _rope_tables(jnp.arange(s, dtype=jnp.int32))
    cos_s, sin_s = _rope_tables(past_len + jnp.arange(t, dtype=jnp.int32))
    cos_s = jnp.tile(cos_s, (tm_s // t, 1))
    sin_s = jnp.tile(sin_s, (tm_s // t, 1))
    tabs_p = _ret_tables(ret_chunk)
    tabs_s = _ret_tables(t)
    perm = _half_split_perm()

    ret_p, ret_s, kp_l, vp_l, ks_l, vs_l, conv_p, conv_s = [], [], [], [], [], [], [], []
    for i in range(depth):
        kind = i % N_MIXERS
        j = i // N_MIXERS
        g_mix = norm_mix[i].reshape(1, d)
        g_ff = norm_ff[i].reshape(1, d)
        if kind == 0:
            w = ret_w_qkvg[j]
            w = jnp.concatenate([w[:, :hk][:, perm], w[:, hk:2 * hk][:, perm], w[:, 2 * hk:]],
                                axis=1).astype(BF16)
            w_o = ret_w_o[j].astype(BF16)
            q, k, v, g = _ret_proj(xp, g_mix, w, cos_p, sin_p, tm=tm_p, qv_dtype=BF16)
            o, st_p = _ret_core(q, k, v, g, tabs_p, None, batch=b, chunk=ret_chunk, nb=1)
            xp = _proj_res(o, w_o, xp, tm=tm_p)
            q, k, v, g = _ret_proj(xs, g_mix, w, cos_s, sin_s, tm=tm_s, qv_dtype=F32)
            o, st_s = _ret_core(q, k, v, g, tabs_s, state_ret[j], batch=db, chunk=t, nb=ret_nb_s)
            xs = _proj_res(o, w_o, xs, tm=tm_s)
            ret_p.append(st_p)
            ret_s.append(st_s)
        elif kind == 1:
            w = moba_w_qkv[j].astype(BF16)
            w_o = moba_w_o[j].astype(BF16)
            q, k, v, kmean = _norm_proj(xp, g_mix, w, (d, d, d), (F32, F32, F32), tm=tm_p,
                                        mean_rows=(1, MOBA_BLOCK))
            o = _moba_prompt(q, k, v, kmean, batch=b)
            xp = _proj_res(o, w_o, xp, tm=tm_p)
            kp_l.append(k.reshape(b, s, MOBA_HEADS, MOBA_HD))
            vp_l.append(v.reshape(b, s, MOBA_HEADS, MOBA_HD))
            q, k, v = _norm_proj(xs, g_mix, w, (d, d, d), (F32, F32, F32), tm=tm_s)
            o = _moba_paged(q.reshape(db, t, d), k.reshape(db, t, d), v.reshape(db, t, d),
                            cache_k[j].reshape(-1, page, d), cache_v[j].reshape(-1, page, d),
                            page_table)
            xs = _proj_res(o.reshape(db * t, d), w_o, xs, tm=tm_s)
            ks_l.append(k.reshape(db, t, MOBA_HEADS, MOBA_HD))
            vs_l.append(v.reshape(db, t, MOBA_HEADS, MOBA_HD))
        else:
            w_in = conv_w_in[j].astype(BF16)
            w_out = conv_w_out[j].astype(BF16)
            xp, buf_p = _conv_mixer(xp, g_mix, w_in, conv_w[j], w_out, tm=tm_p, seq_len=s)
            xs, buf_s = _conv_mixer(xs, g_mix, w_in, conv_w[j], w_out, tm=tm_s, seq_len=t,
                                    state=state_conv[j])
            conv_p.append(buf_p)
            conv_s.append(buf_s)
        w_up = mlp_w_up[i].astype(BF16)
        w_down = mlp_w_down[i].astype(BF16)
        xp = _mlp(xp, g_ff, w_up, w_down, tm=1024, tf=1024)
        xs = _mlp(xs, g_ff, w_up, w_down, tm=tm_s, tf=1024)
    g_fin = norm_final.reshape(1, d)
    y_prompt = _final_norm(xp, g_fin, tm=1024).reshape(b, s, d)
    y_sample = _final_norm(xs, g_fin, tm=tm_s).reshape(db, t, d)
    return (y_prompt, y_sample, jnp.stack(ret_p), jnp.stack(ret_s), jnp.stack(kp_l), jnp.stack(vp_l),
            jnp.stack(ks_l), jnp.stack(vs_l), jnp.stack(conv_p), jnp.stack(conv_s))
```

```python
import functools

import jax
import jax.numpy as jnp
from jax import lax
from jax.experimental import pallas as pl
from jax.experimental.pallas import tpu as pltpu

F32 = jnp.float32
BF16 = jnp.bfloat16
NEG_INF = float("-inf")

D_MODEL = 1024
N_MIXERS = 3
RET_HEADS = 4
RET_DK = D_MODEL // RET_HEADS
RET_DV = 2 * RET_DK
ROPE_BASE = 10000.0
MOBA_HEADS = 16
MOBA_HD = D_MODEL // MOBA_HEADS
MOBA_BLOCK = 256
MOBA_TOPK = 3
CONV_WIDTH = 3
D_FF = 4 * D_MODEL
EPS = 1e-6

LANES = 128
SUBLANES = 8

TM_PROMPT = 512
TM_MLP = 1024
TF_MLP = 1024
RET_CHUNK = 256
RET_SEQS_PER_STEP = 8
PAGES_PER_STEP = 8

_HI = lax.Precision.HIGHEST


def _params(*sem):
    return pltpu.CompilerParams(dimension_semantics=sem)


def _row_tile(m, want):
    t = min(m, want)
    assert m % t == 0, (m, t)
    return t


def _rms(x, gamma):
    ms = jnp.mean(x * x, axis=-1, keepdims=True)
    return x * lax.rsqrt(ms + EPS) * gamma


def _dot(a, b, precision=None):
    return jnp.dot(a, b, preferred_element_type=F32, precision=precision)


def _dot_nt(a, b):
    return lax.dot_general(a, b, (((1,), (1,)), ((), ())), preferred_element_type=F32)


def _dot_tn(a, b):
    return lax.dot_general(a, b, (((0,), (0,)), ((), ())), preferred_element_type=F32)


def _iota(shape, dim):
    return lax.broadcasted_iota(jnp.int32, shape, dim)


def _norm_proj_kernel(x_ref, g_ref, w_ref, *out_refs, widths):
    h = _rms(x_ref[...], g_ref[...]).astype(BF16)
    off = 0
    for idx, wd in enumerate(widths):
        out_refs[idx][...] = _dot(h, w_ref[:, off:off + wd]).astype(out_refs[idx].dtype)
        off += wd


def _norm_proj(x, gamma, w, widths, dtypes, *, tm):
    m, d = x.shape
    tm = _row_tile(m, tm)
    n = sum(widths)
    return pl.pallas_call(
        functools.partial(_norm_proj_kernel, widths=tuple(widths)),
        grid=(m // tm,),
        in_specs=[pl.BlockSpec((tm, d), lambda i: (i, 0)),
                  pl.BlockSpec((1, d), lambda i: (0, 0)),
                  pl.BlockSpec((d, n), lambda i: (0, 0))],
        out_specs=[pl.BlockSpec((tm, wd), lambda i: (i, 0)) for wd in widths],
        out_shape=[jax.ShapeDtypeStruct((m, wd), dt) for wd, dt in zip(widths, dtypes)],
        compiler_params=_params("arbitrary"),
        name="norm_proj",
    )(x, gamma, w)


def _moba_proj_t_kernel(x_ref, g_ref, wt_ref, wk_ref, qt_ref, k_ref, kt_ref, vt_ref, km_ref):
    d = D_MODEL
    h = _rms(x_ref[...], g_ref[...]).astype(BF16)
    qt_ref[0] = _dot_nt(wt_ref[0:d, :], h)
    kt_ref[0] = _dot_nt(wt_ref[d:2 * d, :], h)
    vt_ref[0] = _dot_nt(wt_ref[2 * d:3 * d, :], h)
    k = _dot(h, wk_ref[...])
    k_ref[...] = k.astype(BF16)
    for r in range(k.shape[0] // MOBA_BLOCK):
        km_ref[r] = jnp.mean(k[r * MOBA_BLOCK:(r + 1) * MOBA_BLOCK], axis=0, keepdims=True)


def _moba_proj_t(x, gamma, w_qkv, *, batch, tm):
    m, d = x.shape
    s_len = m // batch
    tm = _row_tile(s_len, tm)
    assert tm % MOBA_BLOCK == 0
    per_seq = s_len // tm
    w_t = w_qkv.T.astype(BF16)
    w_k = w_qkv[:, d:2 * d].astype(BF16)
    feat = lambda i: (i // per_seq, 0, i % per_seq)
    t_shape = jax.ShapeDtypeStruct((batch, d, s_len), F32)
    return pl.pallas_call(
        _moba_proj_t_kernel,
        grid=(m // tm,),
        in_specs=[pl.BlockSpec((tm, d), lambda i: (i, 0)),
                  pl.BlockSpec((1, d), lambda i: (0, 0)),
                  pl.BlockSpec((3 * d, d), lambda i: (0, 0)),
                  pl.BlockSpec((d, d), lambda i: (0, 0))],
        out_specs=[pl.BlockSpec((1, d, tm), feat),
                   pl.BlockSpec((tm, d), lambda i: (i, 0)),
                   pl.BlockSpec((1, d, tm), feat),
                   pl.BlockSpec((1, d, tm), feat),
                   pl.BlockSpec((tm // MOBA_BLOCK, 1, d), lambda i: (i, 0, 0))],
        out_shape=[t_shape, jax.ShapeDtypeStruct((m, d), BF16), t_shape, t_shape,
                   jax.ShapeDtypeStruct((m // MOBA_BLOCK, 1, d), F32)],
        compiler_params=_params("arbitrary"),
        name="moba_proj_t",
    )(x, gamma, w_t, w_k)


def _ret_proj_kernel(x_ref, g_ref, w_ref, cos_ref, sin_ref, q_ref, k_ref, v_ref, gate_ref):
    h = _rms(x_ref[...], g_ref[...]).astype(BF16)
    hk = RET_HEADS * RET_DK
    hv = RET_HEADS * RET_DV
    even = (_iota((1, LANES), 1) % 2) == 0
    for off, scale, out in ((0, 1.0, q_ref), (hk, RET_DK ** -0.5, k_ref)):
        p = _dot(h, w_ref[:, off:off + hk])
        for c in range(hk // LANES):
            x = p[:, c * LANES:(c + 1) * LANES]
            swapped = jnp.where(even, pltpu.roll(x, LANES - 1, 1), pltpu.roll(x, 1, 1))
            tab = pl.ds((c * LANES) % RET_DK, LANES)
            rot = x * cos_ref[:, tab] + swapped * sin_ref[:, tab]
            out[:, c * LANES:(c + 1) * LANES] = (rot * scale).astype(out.dtype)
    v_ref[...] = _dot(h, w_ref[:, 2 * hk:2 * hk + hv]).astype(v_ref.dtype)
    gate_ref[...] = _dot(h, w_ref[:, 2 * hk + hv:])


def _ret_proj(x, gamma, w, cos, sin, *, tm, qv_dtype):
    m, d = x.shape
    tm = _row_tile(m, tm)
    hk = RET_HEADS * RET_DK
    hv = RET_HEADS * RET_DV
    n_tab = cos.shape[0] // tm
    assert cos.shape[0] % tm == 0
    row = lambda i: (i, 0)
    return pl.pallas_call(
        _ret_proj_kernel,
        grid=(m // tm,),
        in_specs=[pl.BlockSpec((tm, d), row),
                  pl.BlockSpec((1, d), lambda i: (0, 0)),
                  pl.BlockSpec((d, 2 * hk + 2 * hv), lambda i: (0, 0)),
                  pl.BlockSpec((tm, RET_DK), lambda i: (i % n_tab, 0)),
                  pl.BlockSpec((tm, RET_DK), lambda i: (i % n_tab, 0))],
        out_specs=[pl.BlockSpec((tm, hk), row), pl.BlockSpec((tm, hk), row),
                   pl.BlockSpec((tm, hv), row), pl.BlockSpec((tm, hv), row)],
        out_shape=[jax.ShapeDtypeStruct((m, hk), qv_dtype), jax.ShapeDtypeStruct((m, hk), F32),
                   jax.ShapeDtypeStruct((m, hv), qv_dtype), jax.ShapeDtypeStruct((m, hv), F32)],
        compiler_params=_params("arbitrary"),
        name="ret_proj",
    )(x, gamma, w, cos, sin)


def _rope_tables(pos):
    inv = ROPE_BASE ** (-jnp.arange(0, RET_DK, 2, dtype=F32) / RET_DK)
    ang = pos.astype(F32)[:, None] * inv[None, :]
    cos = jnp.repeat(jnp.cos(ang), 2, axis=1)
    sin = jnp.sin(ang)
    return cos, jnp.stack([-sin, sin], axis=-1).reshape(cos.shape)


def _ret_core_kernel(*refs, nb, chunk, n_chunks, has_state, layer, fill_layers):
    refs = list(refs)
    q_ref, k_ref, v_ref, g_ref, dm_ref, qd_ref, kd_ref = refs[:7]
    st0_ref = refs[7] if has_state else None
    o_ref, stout_ref, st_scr = refs[-3:]
    c = pl.program_id(2)

    @pl.when(c == 0)
    def _():
        for s in range(nb):
            st_scr[s] = st0_ref[s, 0] if has_state else jnp.zeros((RET_DK, RET_DV), F32)

    dm = dm_ref[0]
    qd = qd_ref[0]
    kd = kd_ref[0]
    qd_v = jnp.concatenate([qd] * (RET_DV // LANES), axis=1)
    kd_k = jnp.concatenate([kd] * (RET_DK // LANES), axis=1)
    sdec = qd[chunk - 1:chunk, 0:1]
    for s in range(nb):
        q = q_ref[s].astype(BF16)
        kf = k_ref[s]
        v = v_ref[s].astype(BF16)
        st = st_scr[s]
        sc = _dot_nt(q, kf.astype(BF16)) * dm
        inner = _dot(sc.astype(BF16), v)
        cross = _dot(q, st.astype(BF16)) * qd_v
        o = inner + cross
        kv = _dot_tn((kf * kd_k).astype(BF16), v)
        st_new = sdec * st + kv
        st_scr[s] = st_new
        on = o * lax.rsqrt(jnp.mean(o * o, axis=-1, keepdims=True) + EPS)
        gg = g_ref[s]
        o_ref[s] = (on * (gg * jax.nn.sigmoid(gg))).astype(o_ref.dtype)

        @pl.when(c == n_chunks - 1)
        def _():
            if fill_layers:
                for jj in range(fill_layers):
                    stout_ref[jj, s, 0] = st_new if jj == layer else jnp.zeros_like(st_new)
            else:
                stout_ref[s, 0] = st_new


def _ret_core(q, k, v, g, tabs, *, batch, chunk, nb, state_all=None, layer=0, out_all=None):
    hk = RET_HEADS * RET_DK
    hv = RET_HEADS * RET_DV
    s_len = q.shape[0] // batch
    n_chunks = s_len // chunk
    assert s_len % chunk == 0 and batch % nb == 0
    dm, qd, kd = tabs
    seq = lambda b, h, c: (b, c, h)
    tab = lambda b, h, c: (h, 0, 0)
    in_specs = [pl.BlockSpec((nb, chunk, RET_DK), seq), pl.BlockSpec((nb, chunk, RET_DK), seq),
                pl.BlockSpec((nb, chunk, RET_DV), seq), pl.BlockSpec((nb, chunk, RET_DV), seq),
                pl.BlockSpec((1, chunk, chunk), tab), pl.BlockSpec((1, chunk, LANES), tab),
                pl.BlockSpec((1, chunk, LANES), tab)]
    args = [q.reshape(batch, s_len, hk), k.reshape(batch, s_len, hk),
            v.reshape(batch, s_len, hv), g.reshape(batch, s_len, hv), dm, qd, kd]
    aliases = {}
    fill_layers = 0
    if state_all is None:
        st_shape = (batch, RET_HEADS, RET_DK, RET_DV)
        st_spec = pl.BlockSpec((nb, 1, RET_DK, RET_DV), lambda b, h, c: (b, h, 0, 0))
    else:
        n_layers = state_all.shape[0]
        st_shape = state_all.shape
        in_specs.append(pl.BlockSpec((None, nb, 1, RET_DK, RET_DV), lambda b, h, c: (layer, b, h, 0, 0)))
        args.append(state_all)
        if out_all is None:
            fill_layers = n_layers
            st_spec = pl.BlockSpec((n_layers, nb, 1, RET_DK, RET_DV), lambda b, h, c: (0, b, h, 0, 0))
        else:
            in_specs.append(pl.BlockSpec(memory_space=pl.ANY))
            args.append(out_all)
            aliases = {len(args) - 1: 1}
            st_spec = pl.BlockSpec((None, nb, 1, RET_DK, RET_DV), lambda b, h, c: (layer, b, h, 0, 0))
    kern = functools.partial(_ret_core_kernel, nb=nb, chunk=chunk, n_chunks=n_chunks,
                             has_state=state_all is not None, layer=layer, fill_layers=fill_layers)
    if aliases:
        inner = kern
        n_in = len(args)
        kern = lambda *refs: inner(*refs[:n_in - 1], *refs[n_in:])
    o, st = pl.pallas_call(
        kern,
        grid=(batch // nb, RET_HEADS, n_chunks),
        in_specs=in_specs,
        out_specs=[pl.BlockSpec((nb, chunk, RET_DV), seq), st_spec],
        out_shape=[jax.ShapeDtypeStruct((batch, s_len, hv), BF16),
                   jax.ShapeDtypeStruct(st_shape, F32)],
        scratch_shapes=[pltpu.VMEM((nb, RET_DK, RET_DV), F32)],
        input_output_aliases=aliases,
        compiler_params=_params("arbitrary", "arbitrary", "arbitrary"),
        name="ret_core",
    )(*args)
    return o.reshape(batch * s_len, hv), st


def _ret_tables(chunk):
    log_g = jnp.log(1.0 - 2.0 ** (-5.0 - jnp.arange(RET_HEADS, dtype=F32)))
    i = jnp.arange(chunk, dtype=F32)
    diff = i[:, None] - i[None, :]
    dm = jnp.where(diff >= 0, jnp.exp(log_g[:, None, None] * jnp.maximum(diff, 0.0)), 0.0)
    qd = jnp.exp(log_g[:, None] * (i[None, :] + 1.0))
    kd = jnp.exp(log_g[:, None] * (chunk - 1.0 - i[None, :]))
    rep = lambda t: jnp.broadcast_to(t[:, :, None], (RET_HEADS, chunk, LANES))
    return dm, rep(qd), rep(kd)


def _proj_res_kernel(a_ref, w_ref, x_ref, o_ref):
    o_ref[...] = x_ref[...] + _dot(a_ref[...].astype(BF16), w_ref[...])


def _proj_res(a, w, x, *, tm):
    m, kdim = a.shape
    d = x.shape[1]
    tm = _row_tile(m, tm)
    return pl.pallas_call(
        _proj_res_kernel,
        grid=(m // tm,),
        in_specs=[pl.BlockSpec((tm, kdim), lambda i: (i, 0)),
                  pl.BlockSpec((kdim, d), lambda i: (0, 0)),
                  pl.BlockSpec((tm, d), lambda i: (i, 0))],
        out_specs=pl.BlockSpec((tm, d), lambda i: (i, 0)),
        out_shape=jax.ShapeDtypeStruct((m, d), F32),
        compiler_params=_params("arbitrary"),
        name="proj_res",
    )(a, w, x)


def _mlp_kernel(x_ref, g_ref, wu_ref, wd_ref, o_ref, h_scr, acc_scr, *, n_f):
    f = pl.program_id(1)

    @pl.when(f == 0)
    def _():
        h_scr[...] = _rms(x_ref[...], g_ref[...]).astype(BF16)

    a = jnp.maximum(_dot(h_scr[...], wu_ref[...]), 0.0)
    part = _dot((a * a).astype(BF16), wd_ref[...])

    @pl.when(f == 0)
    def _():
        acc_scr[...] = part

    @pl.when(f > 0)
    def _():
        acc_scr[...] += part

    @pl.when(f == n_f - 1)
    def _():
        o_ref[...] = x_ref[...] + acc_scr[...]


def _mlp(x, gamma, w_up, w_down, *, tm, tf):
    m, d = x.shape
    tm = _row_tile(m, tm)
    n_f = D_FF // tf
    return pl.pallas_call(
        functools.partial(_mlp_kernel, n_f=n_f),
        grid=(m // tm, n_f),
        in_specs=[pl.BlockSpec((tm, d), lambda i, f: (i, 0)),
                  pl.BlockSpec((1, d), lambda i, f: (0, 0)),
                  pl.BlockSpec((d, tf), lambda i, f: (0, f)),
                  pl.BlockSpec((tf, d), lambda i, f: (f, 0))],
        out_specs=pl.BlockSpec((tm, d), lambda i, f: (i, 0)),
        out_shape=jax.ShapeDtypeStruct((m, d), F32),
        scratch_shapes=[pltpu.VMEM((tm, d), BF16), pltpu.VMEM((tm, d), F32)],
        compiler_params=_params("arbitrary", "arbitrary"),
        name="mlp",
    )(x, gamma, w_up, w_down)


def _conv_kernel(*refs, tiles_per_seq, seq_in_tile):
    if seq_in_tile:
        x_ref, g_ref, win_ref, wc_ref, wout_ref, p1_ref, p2_ref, o_ref, u_ref = refs
    else:
        x_ref, g_ref, win_ref, wc_ref, wout_ref, o_ref, u_ref, carry_scr = refs
    d = D_MODEL
    x = x_ref[...]
    tm = x.shape[0]
    h = _rms(x, g_ref[...]).astype(BF16)
    bg = _dot(h, win_ref[:, 0:d])
    u = _dot(h, win_ref[:, d:2 * d]) * _dot(h, win_ref[:, 2 * d:3 * d])
    row = _iota((tm, 1), 0)
    r1 = pltpu.roll(u, 1, 0)
    r2 = pltpu.roll(u, 2, 0)
    if seq_in_tile:
        t = row % seq_in_tile
        um1 = jnp.where(t >= 1, r1, 0.0) + p1_ref[...]
        um2 = jnp.where(t >= 2, r2, 0.0) + p2_ref[...]
        u_ref[...] = u
    else:
        @pl.when((pl.program_id(0) % tiles_per_seq) == 0)
        def _():
            carry_scr[...] = jnp.zeros((SUBLANES, d), F32)

        carry = carry_scr[...]
        c1 = carry[SUBLANES - 1:SUBLANES]
        c2 = carry[SUBLANES - 2:SUBLANES - 1]
        um1 = jnp.where(row >= 1, r1, c1)
        um2 = jnp.where(row >= 2, r2, jnp.where(row == 1, c1, c2))
        tail = u[tm - SUBLANES:tm]
        carry_scr[...] = tail
        u_ref[...] = tail
    wc = wc_ref[...]
    y = wc[0:1] * um2 + wc[1:2] * um1 + wc[2:3] * u
    o_ref[...] = x + _dot((bg * y).astype(BF16), wout_ref[...])


def _conv_mixer(x, gamma, w_in, w_c, w_out, *, tm, seq_len, state=None):
    m, d = x.shape
    n_seq = m // seq_len
    wc_pad = jnp.zeros((SUBLANES, d), F32).at[:CONV_WIDTH].set(w_c)
    row = lambda i: (i, 0)
    const = lambda i: (0, 0)
    in_specs = [pl.BlockSpec((tm, d), row), pl.BlockSpec((1, d), const),
                pl.BlockSpec((d, 3 * d), const), pl.BlockSpec((SUBLANES, d), const),
                pl.BlockSpec((d, d), const)]
    args = [x, gamma, w_in, wc_pad, w_out]
    if state is not None:
        tm = _row_tile(m, tm)
        assert tm % seq_len == 0 and seq_len >= CONV_WIDTH - 1
        zeros = jnp.zeros((n_seq, seq_len, d), F32)
        p1 = zeros.at[:, 0].set(state[:, 1]).reshape(m, d)
        p2 = zeros.at[:, 0].set(state[:, 0]).at[:, 1].set(state[:, 1]).reshape(m, d)
        in_specs += [pl.BlockSpec((tm, d), row), pl.BlockSpec((tm, d), row)]
        args += [p1, p2]
        u_rows, u_blk, scratch = m, tm, []
        kern = functools.partial(_conv_kernel, tiles_per_seq=0, seq_in_tile=seq_len)
    else:
        tm = _row_tile(seq_len, tm)
        u_rows, u_blk = (m // tm) * SUBLANES, SUBLANES
        scratch = [pltpu.VMEM((SUBLANES, d), F32)]
        kern = functools.partial(_conv_kernel, tiles_per_seq=seq_len // tm, seq_in_tile=0)
    out, u = pl.pallas_call(
        kern,
        grid=(m // tm,),
        in_specs=in_specs,
        out_specs=[pl.BlockSpec((tm, d), row), pl.BlockSpec((u_blk, d), row)],
        out_shape=[jax.ShapeDtypeStruct((m, d), F32), jax.ShapeDtypeStruct((u_rows, d), F32)],
        scratch_shapes=scratch,
        compiler_params=_params("arbitrary"),
        name="conv_mixer",
    )(*args)
    u_last = u.reshape(n_seq, -1, d)[:, -(CONV_WIDTH - 1):]
    return out, u_last


def _topk_bias(gate, blk, valid, axis):
    n = gate.shape[axis]
    g = jnp.where(valid, gate, NEG_INF)
    sel = jnp.zeros(gate.shape, jnp.bool_)
    for _ in range(MOBA_TOPK):
        mx = jnp.max(g, axis=axis, keepdims=True)
        idx = jnp.min(jnp.where(g == mx, blk, n), axis=axis, keepdims=True)
        pick = blk == idx
        sel = sel | (pick & valid)
        g = jnp.where(pick, NEG_INF, g)
    return jnp.where(sel, 0.0, NEG_INF)


def _moba_prompt_kernel(qt_ref, k_ref, vt_ref, km_ref, o_ref, bias_scr, *, nblk):
    i = pl.program_id(2)
    bs = MOBA_BLOCK
    hd = MOBA_HD
    n_h = LANES // hd
    qt = qt_ref[0]
    km = km_ref[...]
    feat_lane = _iota((1, LANES), 1) // hd
    feat_row = _iota((LANES, 1), 0) // hd
    blk = _iota((nblk, 1), 0)
    causal = _iota((bs, 1), 0) <= _iota((1, bs), 1)
    valid = blk < i

    def key_block(j):
        start = pl.multiple_of(j * bs, bs)
        return k_ref[pl.ds(start, bs), :], vt_ref[0, :, pl.ds(start, bs)].astype(BF16)

    k_own, vt_own = key_block(i)
    qbs, carry = [], []
    for h2 in range(n_h):
        gate = _dot(jnp.where(feat_lane == h2, km, 0.0), qt, precision=_HI)
        bias_scr[h2] = _topk_bias(gate, blk, valid, 0)
        qb = jnp.where(feat_row == h2, qt * (hd ** -0.5), 0.0).astype(BF16)
        qbs.append(qb)
        s = jnp.where(causal, _dot(k_own, qb), NEG_INF)
        m = jnp.max(s, axis=0, keepdims=True)
        p = jnp.exp(s - m)
        carry += [m, jnp.sum(p, axis=0, keepdims=True),
                  _dot(vt_own[h2 * hd:(h2 + 1) * hd], p.astype(BF16))]

    def body(j, carry):
        kj, vtj = key_block(j)
        new = []
        for h2 in range(n_h):
            m, l, acc = carry[3 * h2:3 * h2 + 3]
            s = _dot(kj, qbs[h2]) + bias_scr[h2, pl.ds(j, 1), :]
            m_new = jnp.maximum(m, jnp.max(s, axis=0, keepdims=True))
            alpha = jnp.exp(m - m_new)
            p = jnp.exp(s - m_new)
            new += [m_new, alpha * l + jnp.sum(p, axis=0, keepdims=True),
                    alpha * acc + _dot(vtj[h2 * hd:(h2 + 1) * hd], p.astype(BF16))]
        return tuple(new)

    carry = lax.fori_loop(0, i, body, tuple(carry))
    ot = jnp.concatenate([carry[3 * h2 + 2] / carry[3 * h2 + 1] for h2 in range(n_h)], axis=0)
    o_ref[...] = ot.T.astype(o_ref.dtype)


def _moba_prompt(qt, k, vt, kmean):
    batch, d, s_len = qt.shape
    nblk = s_len // MOBA_BLOCK
    assert s_len % MOBA_BLOCK == 0 and nblk % SUBLANES == 0
    n_tiles = d // LANES
    km2 = kmean.reshape(batch * nblk, d)
    return pl.pallas_call(
        functools.partial(_moba_prompt_kernel, nblk=nblk),
        grid=(batch, n_tiles, nblk),
        in_specs=[pl.BlockSpec((1, LANES, MOBA_BLOCK), lambda b, h, i: (b, h, i)),
                  pl.BlockSpec((s_len, LANES), lambda b, h, i: (b, h)),
                  pl.BlockSpec((1, LANES, s_len), lambda b, h, i: (b, h, 0)),
                  pl.BlockSpec((nblk, LANES), lambda b, h, i: (b, h))],
        out_specs=pl.BlockSpec((MOBA_BLOCK, LANES), lambda b, h, i: (b * nblk + i, h)),
        out_shape=jax.ShapeDtypeStruct((batch * s_len, d), BF16),
        scratch_shapes=[pltpu.VMEM((LANES // MOBA_HD, nblk, MOBA_BLOCK), F32)],
        compiler_params=_params("arbitrary", "arbitrary", "arbitrary"),
        name="moba_prompt",
    )(qt, k, vt, km2)


def _moba_paged_kernel(pt_ref, q_ref, kn_ref, vn_ref, *refs, n_pages, page, t_new):
    pg = PAGES_PER_STEP
    k_refs = refs[0:pg]
    v_refs = refs[pg:2 * pg]
    o_ref = refs[2 * pg]
    qbd_scr, qbdf_scr, s_scr, p_scr, gate_scr, acc_scr = refs[2 * pg + 1:]
    del pt_ref
    ph = pl.program_id(1)
    n = pl.program_id(2)
    n_grp = n_pages // pg
    ppb = MOBA_BLOCK // page
    n_blk = n_pages // ppb
    n_col = MOBA_HEADS * t_new
    d = D_MODEL
    blk_lane = _iota((1, LANES), 1)

    @pl.when((ph == 0) & (n == 0))
    def _():
        q = q_ref[0]
        qt = jnp.concatenate([q] * MOBA_HEADS, axis=0)
        bd = _iota((n_col, d), 0) // t_new == _iota((n_col, d), 1) // MOBA_HD
        qbdf = jnp.where(bd, qt, 0.0)
        qbdf_scr[...] = qbdf
        qbd_scr[...] = (qbdf * (MOBA_HD ** -0.5)).astype(BF16)
        gate_scr[...] = jnp.zeros((n_col, LANES), F32)

    @pl.when(ph == 0)
    def _():
        qbd = qbd_scr[...]
        qbdf = qbdf_scr[...]
        gate = gate_scr[...]
        for b2 in range(pg // ppb):
            ksum = None
            for r in range(ppb):
                kp = k_refs[b2 * ppb + r][0]
                s_scr[n * pg + b2 * ppb + r] = _dot(qbd, kp.astype(BF16))
                ksum = kp if ksum is None else ksum + kp
            g_part = _dot(qbdf, ksum, precision=_HI)
            g_col = jnp.sum(g_part, axis=1, keepdims=True) * (1.0 / MOBA_BLOCK)
            gate = jnp.where(blk_lane == n * (pg // ppb) + b2, g_col, gate)
        gate_scr[...] = gate

    @pl.when((ph == 1) & (n == 0))
    def _():
        bias = _topk_bias(gate_scr[...], blk_lane, blk_lane < n_blk, 1)
        qbd = qbd_scr[...]
        s_own = _dot_nt(qbd, kn_ref[0].astype(BF16))
        ok_own = _iota((n_col, t_new), 1) <= _iota((n_col, t_new), 0) % t_new
        s_own = jnp.where(ok_own, s_own, NEG_INF)
        m = jnp.max(s_own, axis=1, keepdims=True)
        for p in range(n_pages):
            b = p // ppb
            m = jnp.maximum(m, jnp.max(s_scr[p] + bias[:, b:b + 1], axis=1, keepdims=True))
        e_own = jnp.exp(s_own - m)
        l = jnp.sum(e_own, axis=1, keepdims=True)
        for p in range(n_pages):
            b = p // ppb
            l = l + jnp.sum(jnp.exp(s_scr[p] + bias[:, b:b + 1] - m), axis=1, keepdims=True)
        inv = 1.0 / l
        for p in range(n_pages):
            b = p // ppb
            p_scr[p] = (jnp.exp(s_scr[p] + bias[:, b:b + 1] - m) * inv).astype(BF16)
        acc_scr[...] = _dot((e_own * inv).astype(BF16), vn_ref[0].astype(BF16))

    @pl.when(ph == 1)
    def _():
        acc = acc_scr[...]
        for r in range(pg):
            acc = acc + _dot_nt(p_scr[n * pg + r], v_refs[r][0].astype(BF16))
        acc_scr[...] = acc

        @pl.when(n == n_grp - 1)
        def _():
            bd = _iota((n_col, d), 0) // t_new == _iota((n_col, d), 1) // MOBA_HD
            a = jnp.where(bd, acc, 0.0)
            out = a[0:t_new]
            for hd in range(1, MOBA_HEADS):
                out = out + a[hd * t_new:(hd + 1) * t_new]
            o_ref[0] = out


def _moba_paged(q, k_new, v_new, cache_kt, cache_vt, page_table):
    db, t_new, d = q.shape
    n_pages = page_table.shape[1]
    page = cache_kt.shape[2]
    pg = PAGES_PER_STEP
    past = n_pages * page
    ppb = MOBA_BLOCK // page
    assert past % MOBA_BLOCK == 0 and MOBA_BLOCK % page == 0 and n_pages % pg == 0 and pg % ppb == 0
    assert t_new <= MOBA_BLOCK and MOBA_TOPK <= past // MOBA_BLOCK <= LANES
    n_grp = n_pages // pg
    n_col = MOBA_HEADS * t_new
    pt = page_table.reshape(-1).astype(jnp.int32)

    def k_map(r):
        return lambda b, ph, n, pt_ref: (
            pt_ref[b * n_pages + jnp.where(ph == 0, n, n_grp - 1) * pg + r], 0, 0)

    def v_map(r):
        return lambda b, ph, n, pt_ref: (
            pt_ref[b * n_pages + jnp.where(ph == 0, 0, n) * pg + r], 0, 0)

    tok = pl.BlockSpec((1, t_new, d), lambda b, ph, n, pt_ref: (b, 0, 0))
    in_specs = [tok, tok, tok]
    in_specs += [pl.BlockSpec((1, d, page), k_map(r)) for r in range(pg)]
    in_specs += [pl.BlockSpec((1, d, page), v_map(r)) for r in range(pg)]
    grid_spec = pltpu.PrefetchScalarGridSpec(
        num_scalar_prefetch=1,
        grid=(db, 2, n_grp),
        in_specs=in_specs,
        out_specs=pl.BlockSpec((1, t_new, d), lambda b, ph, n, pt_ref: (b, 0, 0)),
        scratch_shapes=[pltpu.VMEM((n_col, d), BF16), pltpu.VMEM((n_col, d), F32),
                        pltpu.VMEM((n_pages, n_col, page), F32), pltpu.VMEM((n_pages, n_col, page), BF16),
                        pltpu.VMEM((n_col, LANES), F32), pltpu.VMEM((n_col, d), F32)],
    )
    return pl.pallas_call(
        functools.partial(_moba_paged_kernel, n_pages=n_pages, page=page, t_new=t_new),
        grid_spec=grid_spec,
        out_shape=jax.ShapeDtypeStruct((db, t_new, d), F32),
        compiler_params=_params("arbitrary", "arbitrary", "arbitrary"),
        name="moba_paged",
    )(pt, q, k_new, v_new, *([cache_kt] * pg), *([cache_vt] * pg))


def _final_norm_kernel(x_ref, g_ref, o_ref):
    o_ref[...] = _rms(x_ref[...], g_ref[...])


def _final_norm(x, gamma, *, tm):
    m, d = x.shape
    tm = _row_tile(m, tm)
    return pl.pallas_call(
        _final_norm_kernel,
        grid=(m // tm,),
        in_specs=[pl.BlockSpec((tm, d), lambda i: (i, 0)), pl.BlockSpec((1, d), lambda i: (0, 0))],
        out_specs=pl.BlockSpec((tm, d), lambda i: (i, 0)),
        out_shape=jax.ShapeDtypeStruct((m, d), F32),
        compiler_params=_params("arbitrary"),
        name="final_norm",
    )(x, gamma)


def _feature_major_pages(cache):
    n_phys, page, n_h, hd = cache.shape
    return jnp.transpose(cache, (0, 2, 3, 1)).reshape(n_phys, n_h * hd, page)


def _token_rows(xt, batch):
    _, _, s_len = xt.shape
    return jnp.transpose(xt.reshape(batch, MOBA_HEADS, MOBA_HD, s_len), (0, 3, 1, 2))


def kernel(x_prompt, x_sample, state_ret, cache_k, cache_v, state_conv, page_table, norm_mix, norm_ff, norm_final, ret_w_qkvg, ret_w_o, moba_w_qkv, moba_w_o, conv_w_in, conv_w, conv_w_out, mlp_w_up, mlp_w_down):
    b, s, d = x_prompt.shape
    db, t, _ = x_sample.shape
    depth = norm_mix.shape[0]
    n_pages = page_table.shape[1]
    page = cache_k.shape[2]
    past_len = n_pages * page

    xp = x_prompt.reshape(b * s, d)
    xs = x_sample.reshape(db * t, d)
    tm_p = TM_PROMPT
    tm_s = db * t

    cos_p, sin_p = _rope_tables(jnp.arange(s, dtype=jnp.int32))
    cos_s, sin_s = _rope_tables(past_len + jnp.arange(t, dtype=jnp.int32))
    cos_s = jnp.tile(cos_s, (tm_s // t, 1))
    sin_s = jnp.tile(sin_s, (tm_s // t, 1))
    tabs_p = _ret_tables(RET_CHUNK)
    tabs_s = _ret_tables(t)

    ret_p, ret_s_all, kp_l, vp_l, ks_l, vs_l, conv_p, conv_s = [], None, [], [], [], [], [], []
    for i in range(depth):
        kind = i % N_MIXERS
        j = i // N_MIXERS
        g_mix = norm_mix[i].reshape(1, d)
        g_ff = norm_ff[i].reshape(1, d)
        if kind == 0:
            w = ret_w_qkvg[j].astype(BF16)
            w_o = ret_w_o[j].astype(BF16)
            q, k, v, g = _ret_proj(xp, g_mix, w, cos_p, sin_p, tm=tm_p, qv_dtype=BF16)
            o, st_p = _ret_core(q, k, v, g, tabs_p, batch=b, chunk=RET_CHUNK, nb=1)
            xp = _proj_res(o, w_o, xp, tm=tm_p)
            q, k, v, g = _ret_proj(xs, g_mix, w, cos_s, sin_s, tm=tm_s, qv_dtype=F32)
            o, ret_s_all = _ret_core(q, k, v, g, tabs_s, batch=db, chunk=t, nb=RET_SEQS_PER_STEP,
                                     state_all=state_ret, layer=j, out_all=ret_s_all)
            xs = _proj_res(o, w_o, xs, tm=tm_s)
            ret_p.append(st_p)
        elif kind == 1:
            w_o = moba_w_o[j].astype(BF16)
            qt, k, kt, vt, kmean = _moba_proj_t(xp, g_mix, moba_w_qkv[j], batch=b, tm=tm_p)
            o = _moba_prompt(qt, k, vt, kmean)
            xp = _proj_res(o, w_o, xp, tm=tm_p)
            kp_l.append(_token_rows(kt, b))
            vp_l.append(_token_rows(vt, b))
            q, k, v = _norm_proj(xs, g_mix, moba_w_qkv[j].astype(BF16), (d, d, d), (F32, F32, F32), tm=tm_s)
            o = _moba_paged(q.reshape(db, t, d), k.reshape(db, t, d), v.reshape(db, t, d),
                            _feature_major_pages(cache_k[j]), _feature_major_pages(cache_v[j]), page_table)
            xs = _proj_res(o.reshape(db * t, d), w_o, xs, tm=tm_s)
            ks_l.append(k.reshape(db, t, MOBA_HEADS, MOBA_HD))
            vs_l.append(v.reshape(db, t, MOBA_HEADS, MOBA_HD))
        else:
            w_in = conv_w_in[j].astype(BF16)
            w_out = conv_w_out[j].astype(BF16)
            xp, buf_p = _conv_mixer(xp, g_mix, w_in, conv_w[j], w_out, tm=tm_p, seq_len=s)
            xs, buf_s = _conv_mixer(xs, g_mix, w_in, conv_w[j], w_out, tm=tm_s, seq_len=t,
                                    state=state_conv[j])
            conv_p.append(buf_p)
            conv_s.append(buf_s)
        w_up = mlp_w_up[i].astype(BF16)
        w_down = mlp_w_down[i].astype(BF16)
        xp = _mlp(xp, g_ff, w_up, w_down, tm=TM_MLP, tf=TF_MLP)
        xs = _mlp(xs, g_ff, w_up, w_down, tm=tm_s, tf=TF_MLP)
    g_fin = norm_final.reshape(1, d)
    y_prompt = _final_norm(xp, g_fin, tm=TM_MLP).reshape(b, s, d)
    y_sample = _final_norm(xs, g_fin, tm=tm_s).reshape(db, t, d)
    return (y_prompt, y_sample, jnp.stack(ret_p), ret_s_all, jnp.stack(kp_l), jnp.stack(vp_l),
            jnp.stack(ks_l), jnp.stack(vs_l), jnp.stack(conv_p), jnp.stack(conv_s))
```

```python
import functools

import jax
import jax.numpy as jnp
from jax import lax
from jax.experimental import pallas as pl
from jax.experimental.pallas import tpu as pltpu

F32 = jnp.float32
BF16 = jnp.bfloat16
NEG_INF = float("-inf")

D_MODEL = 1024
N_MIXERS = 3
RET_HEADS = 4
RET_DK = D_MODEL // RET_HEADS
RET_DV = 2 * RET_DK
ROPE_BASE = 10000.0
MOBA_HEADS = 16
MOBA_HD = D_MODEL // MOBA_HEADS
MOBA_BLOCK = 256
MOBA_TOPK = 3
CONV_WIDTH = 3
D_FF = 4 * D_MODEL
EPS = 1e-6

LANES = 128
SUBLANES = 8

TM_PROMPT = 512
TM_MLP = 1024
TF_MLP = 1024
RET_CHUNK = 256
RET_SEQS_PER_STEP = 8
PAGES_PER_STEP = 8
MOBA_TILES_PER_STEP = 2

_HI = lax.Precision.HIGHEST


def _params(*sem):
    return pltpu.CompilerParams(dimension_semantics=sem)


def _row_tile(m, want):
    t = min(m, want)
    assert m % t == 0, (m, t)
    return t


def _rms(x, gamma):
    ms = jnp.mean(x * x, axis=-1, keepdims=True)
    return x * lax.rsqrt(ms + EPS) * gamma


def _dot(a, b, precision=None):
    return jnp.dot(a, b, preferred_element_type=F32, precision=precision)


def _dot_nt(a, b):
    return lax.dot_general(a, b, (((1,), (1,)), ((), ())), preferred_element_type=F32)


def _dot_tn(a, b):
    return lax.dot_general(a, b, (((0,), (0,)), ((), ())), preferred_element_type=F32)


def _iota(shape, dim):
    return lax.broadcasted_iota(jnp.int32, shape, dim)


def _norm_proj_kernel(x_ref, g_ref, w_ref, *out_refs, widths):
    h = _rms(x_ref[...], g_ref[...]).astype(BF16)
    off = 0
    for idx, wd in enumerate(widths):
        out_refs[idx][...] = _dot(h, w_ref[:, off:off + wd]).astype(out_refs[idx].dtype)
        off += wd


def _norm_proj(x, gamma, w, widths, dtypes, *, tm):
    m, d = x.shape
    tm = _row_tile(m, tm)
    n = sum(widths)
    return pl.pallas_call(
        functools.partial(_norm_proj_kernel, widths=tuple(widths)),
        grid=(m // tm,),
        in_specs=[pl.BlockSpec((tm, d), lambda i: (i, 0)),
                  pl.BlockSpec((1, d), lambda i: (0, 0)),
                  pl.BlockSpec((d, n), lambda i: (0, 0))],
        out_specs=[pl.BlockSpec((tm, wd), lambda i: (i, 0)) for wd in widths],
        out_shape=[jax.ShapeDtypeStruct((m, wd), dt) for wd, dt in zip(widths, dtypes)],
        compiler_params=_params("arbitrary"),
        name="norm_proj",
    )(x, gamma, w)


def _moba_proj_t_kernel(x_ref, g_ref, wt_ref, wk_ref, qt_ref, k_ref, kt_ref, vt_ref, km_ref):
    d = D_MODEL
    h = _rms(x_ref[...], g_ref[...]).astype(BF16)
    qt_ref[0] = _dot_nt(wt_ref[0:d, :], h)
    kt_ref[0] = _dot_nt(wt_ref[d:2 * d, :], h)
    vt_ref[0] = _dot_nt(wt_ref[2 * d:3 * d, :], h)
    k = _dot(h, wk_ref[...])
    k_ref[...] = k.astype(BF16)
    for r in range(k.shape[0] // MOBA_BLOCK):
        km_ref[r] = jnp.mean(k[r * MOBA_BLOCK:(r + 1) * MOBA_BLOCK], axis=0, keepdims=True)


def _moba_proj_t(x, gamma, w_qkv, *, batch, tm):
    m, d = x.shape
    s_len = m // batch
    tm = _row_tile(s_len, tm)
    assert tm % MOBA_BLOCK == 0
    per_seq = s_len // tm
    w_t = w_qkv.T.astype(BF16)
    w_k = w_qkv[:, d:2 * d].astype(BF16)
    feat = lambda i: (i // per_seq, 0, i % per_seq)
    t_shape = jax.ShapeDtypeStruct((batch, d, s_len), F32)
    return pl.pallas_call(
        _moba_proj_t_kernel,
        grid=(m // tm,),
        in_specs=[pl.BlockSpec((tm, d), lambda i: (i, 0)),
                  pl.BlockSpec((1, d), lambda i: (0, 0)),
                  pl.BlockSpec((3 * d, d), lambda i: (0, 0)),
                  pl.BlockSpec((d, d), lambda i: (0, 0))],
        out_specs=[pl.BlockSpec((1, d, tm), feat),
                   pl.BlockSpec((tm, d), lambda i: (i, 0)),
                   pl.BlockSpec((1, d, tm), feat),
                   pl.BlockSpec((1, d, tm), feat),
                   pl.BlockSpec((tm // MOBA_BLOCK, 1, d), lambda i: (i, 0, 0))],
        out_shape=[t_shape, jax.ShapeDtypeStruct((m, d), BF16), t_shape, t_shape,
                   jax.ShapeDtypeStruct((m // MOBA_BLOCK, 1, d), F32)],
        compiler_params=_params("arbitrary"),
        name="moba_proj_t",
    )(x, gamma, w_t, w_k)


def _ret_proj_kernel(x_ref, g_ref, w_ref, cos_ref, sin_ref, q_ref, k_ref, v_ref, gate_ref):
    h = _rms(x_ref[...], g_ref[...]).astype(BF16)
    hk = RET_HEADS * RET_DK
    hv = RET_HEADS * RET_DV
    even = (_iota((1, LANES), 1) % 2) == 0
    for off, scale, out in ((0, 1.0, q_ref), (hk, RET_DK ** -0.5, k_ref)):
        p = _dot(h, w_ref[:, off:off + hk])
        for c in range(hk // LANES):
            x = p[:, c * LANES:(c + 1) * LANES]
            swapped = jnp.where(even, pltpu.roll(x, LANES - 1, 1), pltpu.roll(x, 1, 1))
            tab = pl.ds((c * LANES) % RET_DK, LANES)
            rot = x * cos_ref[:, tab] + swapped * sin_ref[:, tab]
            out[:, c * LANES:(c + 1) * LANES] = (rot * scale).astype(out.dtype)
    v_ref[...] = _dot(h, w_ref[:, 2 * hk:2 * hk + hv]).astype(v_ref.dtype)
    gate_ref[...] = _dot(h, w_ref[:, 2 * hk + hv:])


def _ret_proj(x, gamma, w, cos, sin, *, tm, qv_dtype):
    m, d = x.shape
    tm = _row_tile(m, tm)
    hk = RET_HEADS * RET_DK
    hv = RET_HEADS * RET_DV
    n_tab = cos.shape[0] // tm
    assert cos.shape[0] % tm == 0
    row = lambda i: (i, 0)
    return pl.pallas_call(
        _ret_proj_kernel,
        grid=(m // tm,),
        in_specs=[pl.BlockSpec((tm, d), row),
                  pl.BlockSpec((1, d), lambda i: (0, 0)),
                  pl.BlockSpec((d, 2 * hk + 2 * hv), lambda i: (0, 0)),
                  pl.BlockSpec((tm, RET_DK), lambda i: (i % n_tab, 0)),
                  pl.BlockSpec((tm, RET_DK), lambda i: (i % n_tab, 0))],
        out_specs=[pl.BlockSpec((tm, hk), row), pl.BlockSpec((tm, hk), row),
                   pl.BlockSpec((tm, hv), row), pl.BlockSpec((tm, hv), row)],
        out_shape=[jax.ShapeDtypeStruct((m, hk), qv_dtype), jax.ShapeDtypeStruct((m, hk), F32),
                   jax.ShapeDtypeStruct((m, hv), qv_dtype), jax.ShapeDtypeStruct((m, hv), F32)],
        compiler_params=_params("arbitrary"),
        name="ret_proj",
    )(x, gamma, w, cos, sin)


def _rope_tables(pos):
    inv = ROPE_BASE ** (-jnp.arange(0, RET_DK, 2, dtype=F32) / RET_DK)
    ang = pos.astype(F32)[:, None] * inv[None, :]
    cos = jnp.repeat(jnp.cos(ang), 2, axis=1)
    sin = jnp.sin(ang)
    return cos, jnp.stack([-sin, sin], axis=-1).reshape(cos.shape)


def _ret_core_kernel(*refs, nb, chunk, n_chunks, has_state, layer, fill_layers):
    refs = list(refs)
    q_ref, k_ref, v_ref, g_ref, dm_ref, qd_ref, kd_ref = refs[:7]
    st0_ref = refs[7] if has_state else None
    o_ref, stout_ref, st_scr = refs[-3:]
    c = pl.program_id(2)

    @pl.when(c == 0)
    def _():
        for s in range(nb):
            st_scr[s] = st0_ref[s, 0] if has_state else jnp.zeros((RET_DK, RET_DV), F32)

    dm = dm_ref[0]
    qd = qd_ref[0]
    kd = kd_ref[0]
    qd_v = jnp.concatenate([qd] * (RET_DV // LANES), axis=1)
    kd_k = jnp.concatenate([kd] * (RET_DK // LANES), axis=1)
    sdec = qd[chunk - 1:chunk, 0:1]
    for s in range(nb):
        q = q_ref[s].astype(BF16)
        kf = k_ref[s]
        v = v_ref[s].astype(BF16)
        st = st_scr[s]
        sc = _dot_nt(q, kf.astype(BF16)) * dm
        inner = _dot(sc.astype(BF16), v)
        cross = _dot(q, st.astype(BF16)) * qd_v
        o = inner + cross
        kv = _dot_tn((kf * kd_k).astype(BF16), v)
        st_new = sdec * st + kv
        st_scr[s] = st_new
        on = o * lax.rsqrt(jnp.mean(o * o, axis=-1, keepdims=True) + EPS)
        gg = g_ref[s]
        o_ref[s] = (on * (gg * jax.nn.sigmoid(gg))).astype(o_ref.dtype)

        @pl.when(c == n_chunks - 1)
        def _():
            if fill_layers:
                for jj in range(fill_layers):
                    stout_ref[jj, s, 0] = st_new if jj == layer else jnp.zeros_like(st_new)
            else:
                stout_ref[s, 0] = st_new


def _ret_core(q, k, v, g, tabs, *, batch, chunk, nb, state_all=None, layer=0, out_all=None):
    hk = RET_HEADS * RET_DK
    hv = RET_HEADS * RET_DV
    s_len = q.shape[0] // batch
    n_chunks = s_len // chunk
    assert s_len % chunk == 0 and batch % nb == 0
    dm, qd, kd = tabs
    seq = lambda b, h, c: (b, c, h)
    tab = lambda b, h, c: (h, 0, 0)
    in_specs = [pl.BlockSpec((nb, chunk, RET_DK), seq), pl.BlockSpec((nb, chunk, RET_DK), seq),
                pl.BlockSpec((nb, chunk, RET_DV), seq), pl.BlockSpec((nb, chunk, RET_DV), seq),
                pl.BlockSpec((1, chunk, chunk), tab), pl.BlockSpec((1, chunk, LANES), tab),
                pl.BlockSpec((1, chunk, LANES), tab)]
    args = [q.reshape(batch, s_len, hk), k.reshape(batch, s_len, hk),
            v.reshape(batch, s_len, hv), g.reshape(batch, s_len, hv), dm, qd, kd]
    aliases = {}
    fill_layers = 0
    if state_all is None:
        st_shape = (batch, RET_HEADS, RET_DK, RET_DV)
        st_spec = pl.BlockSpec((nb, 1, RET_DK, RET_DV), lambda b, h, c: (b, h, 0, 0))
    else:
        n_layers = state_all.shape[0]
        st_shape = state_all.shape
        in_specs.append(pl.BlockSpec((None, nb, 1, RET_DK, RET_DV), lambda b, h, c: (layer, b, h, 0, 0)))
        args.append(state_all)
        if out_all is None:
            fill_layers = n_layers
            st_spec = pl.BlockSpec((n_layers, nb, 1, RET_DK, RET_DV), lambda b, h, c: (0, b, h, 0, 0))
        else:
            in_specs.append(pl.BlockSpec(memory_space=pl.ANY))
            args.append(out_all)
            aliases = {len(args) - 1: 1}
            st_spec = pl.BlockSpec((None, nb, 1, RET_DK, RET_DV), lambda b, h, c: (layer, b, h, 0, 0))
    kern = functools.partial(_ret_core_kernel, nb=nb, chunk=chunk, n_chunks=n_chunks,
                             has_state=state_all is not None, layer=layer, fill_layers=fill_layers)
    if aliases:
        inner = kern
        n_in = len(args)
        kern = lambda *refs: inner(*refs[:n_in - 1], *refs[n_in:])
    o, st = pl.pallas_call(
        kern,
        grid=(batch // nb, RET_HEADS, n_chunks),
        in_specs=in_specs,
        out_specs=[pl.BlockSpec((nb, chunk, RET_DV), seq), st_spec],
        out_shape=[jax.ShapeDtypeStruct((batch, s_len, hv), BF16),
                   jax.ShapeDtypeStruct(st_shape, F32)],
        scratch_shapes=[pltpu.VMEM((nb, RET_DK, RET_DV), F32)],
        input_output_aliases=aliases,
        compiler_params=_params("arbitrary", "arbitrary", "arbitrary"),
        name="ret_core",
    )(*args)
    return o.reshape(batch * s_len, hv), st


def _ret_tables(chunk):
    log_g = jnp.log(1.0 - 2.0 ** (-5.0 - jnp.arange(RET_HEADS, dtype=F32)))
    i = jnp.arange(chunk, dtype=F32)
    diff = i[:, None] - i[None, :]
    dm = jnp.where(diff >= 0, jnp.exp(log_g[:, None, None] * jnp.maximum(diff, 0.0)), 0.0)
    qd = jnp.exp(log_g[:, None] * (i[None, :] + 1.0))
    kd = jnp.exp(log_g[:, None] * (chunk - 1.0 - i[None, :]))
    rep = lambda t: jnp.broadcast_to(t[:, :, None], (RET_HEADS, chunk, LANES))
    return dm, rep(qd), rep(kd)


def _proj_res_kernel(a_ref, w_ref, x_ref, o_ref):
    o_ref[...] = x_ref[...] + _dot(a_ref[...].astype(BF16), w_ref[...])


def _proj_res(a, w, x, *, tm):
    m, kdim = a.shape
    d = x.shape[1]
    tm = _row_tile(m, tm)
    return pl.pallas_call(
        _proj_res_kernel,
        grid=(m // tm,),
        in_specs=[pl.BlockSpec((tm, kdim), lambda i: (i, 0)),
                  pl.BlockSpec((kdim, d), lambda i: (0, 0)),
                  pl.BlockSpec((tm, d), lambda i: (i, 0))],
        out_specs=pl.BlockSpec((tm, d), lambda i: (i, 0)),
        out_shape=jax.ShapeDtypeStruct((m, d), F32),
        compiler_params=_params("arbitrary"),
        name="proj_res",
    )(a, w, x)


def _mlp_kernel(x_ref, g_ref, wu_ref, wd_ref, o_ref, h_scr, acc_scr, *, n_f):
    f = pl.program_id(1)

    @pl.when(f == 0)
    def _():
        h_scr[...] = _rms(x_ref[...], g_ref[...]).astype(BF16)

    a = jnp.maximum(_dot(h_scr[...], wu_ref[...]), 0.0)
    part = _dot((a * a).astype(BF16), wd_ref[...])

    @pl.when(f == 0)
    def _():
        acc_scr[...] = part

    @pl.when(f > 0)
    def _():
        acc_scr[...] += part

    @pl.when(f == n_f - 1)
    def _():
        o_ref[...] = x_ref[...] + acc_scr[...]


def _mlp(x, gamma, w_up, w_down, *, tm, tf):
    m, d = x.shape
    tm = _row_tile(m, tm)
    n_f = D_FF // tf
    return pl.pallas_call(
        functools.partial(_mlp_kernel, n_f=n_f),
        grid=(m // tm, n_f),
        in_specs=[pl.BlockSpec((tm, d), lambda i, f: (i, 0)),
                  pl.BlockSpec((1, d), lambda i, f: (0, 0)),
                  pl.BlockSpec((d, tf), lambda i, f: (0, f)),
                  pl.BlockSpec((tf, d), lambda i, f: (f, 0))],
        out_specs=pl.BlockSpec((tm, d), lambda i, f: (i, 0)),
        out_shape=jax.ShapeDtypeStruct((m, d), F32),
        scratch_shapes=[pltpu.VMEM((tm, d), BF16), pltpu.VMEM((tm, d), F32)],
        compiler_params=_params("arbitrary", "arbitrary"),
        name="mlp",
    )(x, gamma, w_up, w_down)


def _conv_kernel(*refs, tiles_per_seq, seq_in_tile):
    if seq_in_tile:
        x_ref, g_ref, win_ref, wc_ref, wout_ref, p1_ref, p2_ref, o_ref, u_ref = refs
    else:
        x_ref, g_ref, win_ref, wc_ref, wout_ref, o_ref, u_ref, carry_scr = refs
    d = D_MODEL
    x = x_ref[...]
    tm = x.shape[0]
    h = _rms(x, g_ref[...]).astype(BF16)
    bg = _dot(h, win_ref[:, 0:d])
    u = _dot(h, win_ref[:, d:2 * d]) * _dot(h, win_ref[:, 2 * d:3 * d])
    row = _iota((tm, 1), 0)
    r1 = pltpu.roll(u, 1, 0)
    r2 = pltpu.roll(u, 2, 0)
    if seq_in_tile:
        t = row % seq_in_tile
        um1 = jnp.where(t >= 1, r1, 0.0) + p1_ref[...]
        um2 = jnp.where(t >= 2, r2, 0.0) + p2_ref[...]
        u_ref[...] = u
    else:
        @pl.when((pl.program_id(0) % tiles_per_seq) == 0)
        def _():
            carry_scr[...] = jnp.zeros((SUBLANES, d), F32)

        carry = carry_scr[...]
        c1 = carry[SUBLANES - 1:SUBLANES]
        c2 = carry[SUBLANES - 2:SUBLANES - 1]
        um1 = jnp.where(row >= 1, r1, c1)
        um2 = jnp.where(row >= 2, r2, jnp.where(row == 1, c1, c2))
        tail = u[tm - SUBLANES:tm]
        carry_scr[...] = tail
        u_ref[...] = tail
    wc = wc_ref[...]
    y = wc[0:1] * um2 + wc[1:2] * um1 + wc[2:3] * u
    o_ref[...] = x + _dot((bg * y).astype(BF16), wout_ref[...])


def _conv_mixer(x, gamma, w_in, w_c, w_out, *, tm, seq_len, state=None):
    m, d = x.shape
    n_seq = m // seq_len
    wc_pad = jnp.zeros((SUBLANES, d), F32).at[:CONV_WIDTH].set(w_c)
    row = lambda i: (i, 0)
    const = lambda i: (0, 0)
    in_specs = [pl.BlockSpec((tm, d), row), pl.BlockSpec((1, d), const),
                pl.BlockSpec((d, 3 * d), const), pl.BlockSpec((SUBLANES, d), const),
                pl.BlockSpec((d, d), const)]
    args = [x, gamma, w_in, wc_pad, w_out]
    if state is not None:
        tm = _row_tile(m, tm)
        assert tm % seq_len == 0 and seq_len >= CONV_WIDTH - 1
        zeros = jnp.zeros((n_seq, seq_len, d), F32)
        p1 = zeros.at[:, 0].set(state[:, 1]).reshape(m, d)
        p2 = zeros.at[:, 0].set(state[:, 0]).at[:, 1].set(state[:, 1]).reshape(m, d)
        in_specs += [pl.BlockSpec((tm, d), row), pl.BlockSpec((tm, d), row)]
        args += [p1, p2]
        u_rows, u_blk, scratch = m, tm, []
        kern = functools.partial(_conv_kernel, tiles_per_seq=0, seq_in_tile=seq_len)
    else:
        tm = _row_tile(seq_len, tm)
        u_rows, u_blk = (m // tm) * SUBLANES, SUBLANES
        scratch = [pltpu.VMEM((SUBLANES, d), F32)]
        kern = functools.partial(_conv_kernel, tiles_per_seq=seq_len // tm, seq_in_tile=0)
    out, u = pl.pallas_call(
        kern,
        grid=(m // tm,),
        in_specs=in_specs,
        out_specs=[pl.BlockSpec((tm, d), row), pl.BlockSpec((u_blk, d), row)],
        out_shape=[jax.ShapeDtypeStruct((m, d), F32), jax.ShapeDtypeStruct((u_rows, d), F32)],
        scratch_shapes=scratch,
        compiler_params=_params("arbitrary"),
        name="conv_mixer",
    )(*args)
    u_last = u.reshape(n_seq, -1, d)[:, -(CONV_WIDTH - 1):]
    return out, u_last


def _topk_bias(gate, blk, valid, axis):
    n = gate.shape[axis]
    g = jnp.where(valid, gate, NEG_INF)
    sel = jnp.zeros(gate.shape, jnp.bool_)
    for _ in range(MOBA_TOPK):
        mx = jnp.max(g, axis=axis, keepdims=True)
        idx = jnp.min(jnp.where(g == mx, blk, n), axis=axis, keepdims=True)
        pick = blk == idx
        sel = sel | (pick & valid)
        g = jnp.where(pick, NEG_INF, g)
    return jnp.where(sel, 0.0, NEG_INF)


def _moba_prompt_kernel(qt_ref, k_ref, vt_ref, km_ref, o_ref, bias_scr, *, nblk, n_tiles):
    i = pl.program_id(2)
    bs = MOBA_BLOCK
    hd = MOBA_HD
    per_tile = LANES // hd
    heads = [(t, h2) for t in range(n_tiles) for h2 in range(per_tile)]
    feat_lane = _iota((1, LANES), 1) // hd
    feat_row = _iota((LANES, 1), 0) // hd
    blk = _iota((nblk, 1), 0)
    causal = _iota((bs, 1), 0) <= _iota((1, bs), 1)
    valid = blk < i

    def key_block(j):
        start = pl.multiple_of(j * bs, bs)
        return k_ref[pl.ds(start, bs), :], vt_ref[0, :, pl.ds(start, bs)].astype(BF16)

    def scores(kj, qbs):
        return [_dot(kj[:, t * LANES:(t + 1) * LANES], qb) for (t, _), qb in zip(heads, qbs)]

    def values(vtj, ps):
        return [_dot(vtj[n * hd:(n + 1) * hd], p.astype(BF16)) for n, p in enumerate(ps)]

    qbs = []
    for n, (t, h2) in enumerate(heads):
        qt = qt_ref[0, t * LANES:(t + 1) * LANES, :]
        km = km_ref[:, t * LANES:(t + 1) * LANES]
        gate = _dot(jnp.where(feat_lane == h2, km, 0.0), qt, precision=_HI)
        bias_scr[n] = _topk_bias(gate, blk, valid, 0)
        qbs.append(jnp.where(feat_row == h2, qt * (hd ** -0.5), 0.0).astype(BF16))

    k_own, vt_own = key_block(i)
    ss = [jnp.where(causal, s, NEG_INF) for s in scores(k_own, qbs)]
    ms = [jnp.max(s, axis=0, keepdims=True) for s in ss]
    ps = [jnp.exp(s - m) for s, m in zip(ss, ms)]
    ls = [jnp.sum(p, axis=0, keepdims=True) for p in ps]
    accs = values(vt_own, ps)

    def body(j, carry):
        ms, ls, accs = carry
        kj, vtj = key_block(j)
        ss = [s + bias_scr[n, pl.ds(j, 1), :] for n, s in enumerate(scores(kj, qbs))]
        m_new = [jnp.maximum(m, jnp.max(s, axis=0, keepdims=True)) for m, s in zip(ms, ss)]
        alphas = [jnp.exp(m - mn) for m, mn in zip(ms, m_new)]
        ps = [jnp.exp(s - mn) for s, mn in zip(ss, m_new)]
        ls = [a * l + jnp.sum(p, axis=0, keepdims=True) for a, l, p in zip(alphas, ls, ps)]
        accs = [a * acc + pv for a, acc, pv in zip(alphas, accs, values(vtj, ps))]
        return m_new, ls, accs

    ms, ls, accs = lax.fori_loop(0, i, body, (ms, ls, accs))
    ot = jnp.concatenate([acc / l for acc, l in zip(accs, ls)], axis=0)
    o_ref[...] = ot.T.astype(o_ref.dtype)


def _moba_prompt(qt, k, vt, kmean):
    batch, d, s_len = qt.shape
    nblk = s_len // MOBA_BLOCK
    assert s_len % MOBA_BLOCK == 0 and nblk % SUBLANES == 0
    n_tiles = MOBA_TILES_PER_STEP
    width = n_tiles * LANES
    km2 = kmean.reshape(batch * nblk, d)
    return pl.pallas_call(
        functools.partial(_moba_prompt_kernel, nblk=nblk, n_tiles=n_tiles),
        grid=(batch, d // width, nblk),
        in_specs=[pl.BlockSpec((1, width, MOBA_BLOCK), lambda b, h, i: (b, h, i)),
                  pl.BlockSpec((s_len, width), lambda b, h, i: (b, h)),
                  pl.BlockSpec((1, width, s_len), lambda b, h, i: (b, h, 0)),
                  pl.BlockSpec((nblk, width), lambda b, h, i: (b, h))],
        out_specs=pl.BlockSpec((MOBA_BLOCK, width), lambda b, h, i: (b * nblk + i, h)),
        out_shape=jax.ShapeDtypeStruct((batch * s_len, d), BF16),
        scratch_shapes=[pltpu.VMEM((width // MOBA_HD, nblk, MOBA_BLOCK), F32)],
        compiler_params=_params("arbitrary", "arbitrary", "arbitrary"),
        name="moba_prompt",
    )(qt, k, vt, km2)


def _moba_paged_kernel(pt_ref, q_ref, kn_ref, vn_ref, *refs, n_pages, page, t_new):
    pg = PAGES_PER_STEP
    k_refs = refs[0:pg]
    v_refs = refs[pg:2 * pg]
    o_ref = refs[2 * pg]
    qbd_scr, qbdf_scr, s_scr, p_scr, kmean_scr, acc_scr = refs[2 * pg + 1:]
    del pt_ref
    ph = pl.program_id(1)
    n = pl.program_id(2)
    n_grp = n_pages // pg
    ppb = MOBA_BLOCK // page
    n_blk = n_pages // ppb
    n_col = MOBA_HEADS * t_new
    d = D_MODEL
    blk_lane = _iota((1, LANES), 1)

    @pl.when((ph == 0) & (n == 0))
    def _():
        q = q_ref[0]
        qt = jnp.concatenate([q] * MOBA_HEADS, axis=0)
        bd = _iota((n_col, d), 0) // t_new == _iota((n_col, d), 1) // MOBA_HD
        qbdf = jnp.where(bd, qt, 0.0)
        qbdf_scr[...] = qbdf
        qbd_scr[...] = (qbdf * (MOBA_HD ** -0.5)).astype(BF16)
        kmean_scr[...] = jnp.zeros((d, LANES), F32)

    @pl.when(ph == 0)
    def _():
        qbd = qbd_scr[...]
        for b2 in range(pg // ppb):
            ksum = None
            for r in range(ppb):
                kp = k_refs[b2 * ppb + r][0]
                s_scr[n * pg + b2 * ppb + r] = _dot(qbd, kp.astype(BF16))
                ksum = kp if ksum is None else ksum + kp
            kmean = jnp.sum(ksum, axis=1, keepdims=True) * (1.0 / MOBA_BLOCK)
            lane_hit = blk_lane == n * (pg // ppb) + b2
            kmean_scr[...] = jnp.where(lane_hit, kmean, kmean_scr[...])

    @pl.when((ph == 1) & (n == 0))
    def _():
        gate = _dot(qbdf_scr[...], kmean_scr[...], precision=_HI)
        bias = _topk_bias(gate, blk_lane, blk_lane < n_blk, 1)
        qbd = qbd_scr[...]
        s_own = _dot_nt(qbd, kn_ref[0].astype(BF16))
        ok_own = _iota((n_col, t_new), 1) <= _iota((n_col, t_new), 0) % t_new
        s_own = jnp.where(ok_own, s_own, NEG_INF)
        m = jnp.max(s_own, axis=1, keepdims=True)
        for p in range(n_pages):
            b = p // ppb
            m = jnp.maximum(m, jnp.max(s_scr[p] + bias[:, b:b + 1], axis=1, keepdims=True))
        e_own = jnp.exp(s_own - m)
        l = jnp.sum(e_own, axis=1, keepdims=True)
        for p in range(n_pages):
            b = p // ppb
            l = l + jnp.sum(jnp.exp(s_scr[p] + bias[:, b:b + 1] - m), axis=1, keepdims=True)
        inv = 1.0 / l
        for p in range(n_pages):
            b = p // ppb
            p_scr[p] = (jnp.exp(s_scr[p] + bias[:, b:b + 1] - m) * inv).astype(BF16)
        acc_scr[...] = _dot((e_own * inv).astype(BF16), vn_ref[0].astype(BF16))

    @pl.when(ph == 1)
    def _():
        acc = acc_scr[...]
        for r in range(pg):
            acc = acc + _dot_nt(p_scr[n * pg + r], v_refs[r][0].astype(BF16))
        acc_scr[...] = acc

        @pl.when(n == n_grp - 1)
        def _():
            bd = _iota((n_col, d), 0) // t_new == _iota((n_col, d), 1) // MOBA_HD
            a = jnp.where(bd, acc, 0.0)
            out = a[0:t_new]
            for hd in range(1, MOBA_HEADS):
                out = out + a[hd * t_new:(hd + 1) * t_new]
            o_ref[0] = out


def _moba_paged(q, k_new, v_new, cache_kt, cache_vt, page_table):
    db, t_new, d = q.shape
    n_pages = page_table.shape[1]
    page = cache_kt.shape[2]
    pg = PAGES_PER_STEP
    past = n_pages * page
    ppb = MOBA_BLOCK // page
    assert past % MOBA_BLOCK == 0 and MOBA_BLOCK % page == 0 and n_pages % pg == 0 and pg % ppb == 0
    assert t_new <= MOBA_BLOCK and MOBA_TOPK <= past // MOBA_BLOCK <= LANES
    n_grp = n_pages // pg
    n_col = MOBA_HEADS * t_new
    pt = page_table.reshape(-1).astype(jnp.int32)

    def k_map(r):
        return lambda b, ph, n, pt_ref: (
            pt_ref[b * n_pages + jnp.where(ph == 0, n, n_grp - 1) * pg + r], 0, 0)

    def v_map(r):
        return lambda b, ph, n, pt_ref: (
            pt_ref[b * n_pages + jnp.where(ph == 0, 0, n) * pg + r], 0, 0)

    tok = pl.BlockSpec((1, t_new, d), lambda b, ph, n, pt_ref: (b, 0, 0))
    in_specs = [tok, tok, tok]
    in_specs += [pl.BlockSpec((1, d, page), k_map(r)) for r in range(pg)]
    in_specs += [pl.BlockSpec((1, d, page), v_map(r)) for r in range(pg)]
    grid_spec = pltpu.PrefetchScalarGridSpec(
        num_scalar_prefetch=1,
        grid=(db, 2, n_grp),
        in_specs=in_specs,
        out_specs=pl.BlockSpec((1, t_new, d), lambda b, ph, n, pt_ref: (b, 0, 0)),
        scratch_shapes=[pltpu.VMEM((n_col, d), BF16), pltpu.VMEM((n_col, d), F32),
                        pltpu.VMEM((n_pages, n_col, page), F32), pltpu.VMEM((n_pages, n_col, page), BF16),
                        pltpu.VMEM((d, LANES), F32), pltpu.VMEM((n_col, d), F32)],
    )
    return pl.pallas_call(
        functools.partial(_moba_paged_kernel, n_pages=n_pages, page=page, t_new=t_new),
        grid_spec=grid_spec,
        out_shape=jax.ShapeDtypeStruct((db, t_new, d), F32),
        compiler_params=_params("arbitrary", "arbitrary", "arbitrary"),
        name="moba_paged",
    )(pt, q, k_new, v_new, *([cache_kt] * pg), *([cache_vt] * pg))


def _final_norm_kernel(x_ref, g_ref, o_ref):
    o_ref[...] = _rms(x_ref[...], g_ref[...])


def _final_norm(x, gamma, *, tm):
    m, d = x.shape
    tm = _row_tile(m, tm)
    return pl.pallas_call(
        _final_norm_kernel,
        grid=(m // tm,),
        in_specs=[pl.BlockSpec((tm, d), lambda i: (i, 0)), pl.BlockSpec((1, d), lambda i: (0, 0))],
        out_specs=pl.BlockSpec((tm, d), lambda i: (i, 0)),
        out_shape=jax.ShapeDtypeStruct((m, d), F32),
        compiler_params=_params("arbitrary"),
        name="final_norm",
    )(x, gamma)


def _feature_major_pages(cache):
    n_phys, page, n_h, hd = cache.shape
    return jnp.transpose(cache, (0, 2, 3, 1)).reshape(n_phys, n_h * hd, page)


def _token_rows(xt, batch):
    _, _, s_len = xt.shape
    return jnp.transpose(xt.reshape(batch, MOBA_HEADS, MOBA_HD, s_len), (0, 3, 1, 2))


def kernel(x_prompt, x_sample, state_ret, cache_k, cache_v, state_conv, page_table, norm_mix, norm_ff, norm_final, ret_w_qkvg, ret_w_o, moba_w_qkv, moba_w_o, conv_w_in, conv_w, conv_w_out, mlp_w_up, mlp_w_down):
    b, s, d = x_prompt.shape
    db, t, _ = x_sample.shape
    depth = norm_mix.shape[0]
    n_pages = page_table.shape[1]
    page = cache_k.shape[2]
    past_len = n_pages * page

    xp = x_prompt.reshape(b * s, d)
    xs = x_sample.reshape(db * t, d)
    tm_p = TM_PROMPT
    tm_s = db * t

    cos_p, sin_p = _rope_tables(jnp.arange(s, dtype=jnp.int32))
    cos_s, sin_s = _rope_tables(past_len + jnp.arange(t, dtype=jnp.int32))
    cos_s = jnp.tile(cos_s, (tm_s // t, 1))
    sin_s = jnp.tile(sin_s, (tm_s // t, 1))
    tabs_p = _ret_tables(RET_CHUNK)
    tabs_s = _ret_tables(t)

    ret_p, ret_s_all, kp_l, vp_l, ks_l, vs_l, conv_p, conv_s = [], None, [], [], [], [], [], []
    for i in range(depth):
        kind = i % N_MIXERS
        j = i // N_MIXERS
        g_mix = norm_mix[i].reshape(1, d)
        g_ff = norm_ff[i].reshape(1, d)
        if kind == 0:
            w = ret_w_qkvg[j].astype(BF16)
            w_o = ret_w_o[j].astype(BF16)
            q, k, v, g = _ret_proj(xp, g_mix, w, cos_p, sin_p, tm=tm_p, qv_dtype=BF16)
            o, st_p = _ret_core(q, k, v, g, tabs_p, batch=b, chunk=RET_CHUNK, nb=1)
            xp = _proj_res(o, w_o, xp, tm=tm_p)
            q, k, v, g = _ret_proj(xs, g_mix, w, cos_s, sin_s, tm=tm_s, qv_dtype=F32)
            o, ret_s_all = _ret_core(q, k, v, g, tabs_s, batch=db, chunk=t, nb=RET_SEQS_PER_STEP,
                                     state_all=state_ret, layer=j, out_all=ret_s_all)
            xs = _proj_res(o, w_o, xs, tm=tm_s)
            ret_p.append(st_p)
        elif kind == 1:
            w_o = moba_w_o[j].astype(BF16)
            qt, k, kt, vt, kmean = _moba_proj_t(xp, g_mix, moba_w_qkv[j], batch=b, tm=tm_p)
            o = _moba_prompt(qt, k, vt, kmean)
            xp = _proj_res(o, w_o, xp, tm=tm_p)
            kp_l.append(_token_rows(kt, b))
            vp_l.append(_token_rows(vt, b))
            q, k, v = _norm_proj(xs, g_mix, moba_w_qkv[j].astype(BF16), (d, d, d), (F32, F32, F32), tm=tm_s)
            o = _moba_paged(q.reshape(db, t, d), k.reshape(db, t, d), v.reshape(db, t, d),
                            _feature_major_pages(cache_k[j]), _feature_major_pages(cache_v[j]), page_table)
            xs = _proj_res(o.reshape(db * t, d), w_o, xs, tm=tm_s)
            ks_l.append(k.reshape(db, t, MOBA_HEADS, MOBA_HD))
            vs_l.append(v.reshape(db, t, MOBA_HEADS, MOBA_HD))
        else:
            w_in = conv_w_in[j].astype(BF16)
            w_out = conv_w_out[j].astype(BF16)
            xp, buf_p = _conv_mixer(xp, g_mix, w_in, conv_w[j], w_out, tm=tm_p, seq_len=s)
            xs, buf_s = _conv_mixer(xs, g_mix, w_in, conv_w[j], w_out, tm=tm_s, seq_len=t,
                                    state=state_conv[j])
            conv_p.append(buf_p)
            conv_s.append(buf_s)
        w_up = mlp_w_up[i].astype(BF16)
        w_down = mlp_w_down[i].astype(BF16)
        xp = _mlp(xp, g_ff, w_up, w_down, tm=TM_MLP, tf=TF_MLP)
        xs = _mlp(xs, g_ff, w_up, w_down, tm=tm_s, tf=TF_MLP)
    g_fin = norm_final.reshape(1, d)
    y_prompt = _final_norm(xp, g_fin, tm=TM_MLP).reshape(b, s, d)
    y_sample = _final_norm(xs, g_fin, tm=tm_s).reshape(db, t, d)
    return (y_prompt, y_sample, jnp.stack(ret_p), ret_s_all, jnp.stack(kp_l), jnp.stack(vp_l),
            jnp.stack(ks_l), jnp.stack(vs_l), jnp.stack(conv_p), jnp.stack(conv_s))
```

```python
import functools

import jax
import jax.numpy as jnp
from jax import lax
from jax.experimental import pallas as pl
from jax.experimental.pallas import tpu as pltpu

F32 = jnp.float32
BF16 = jnp.bfloat16
NEG_INF = float("-inf")

D_MODEL = 1024
N_MIXERS = 3
RET_HEADS = 4
RET_DK = D_MODEL // RET_HEADS
RET_DV = 2 * RET_DK
ROPE_BASE = 10000.0
MOBA_HEADS = 16
MOBA_HD = D_MODEL // MOBA_HEADS
MOBA_BLOCK = 256
MOBA_TOPK = 3
CONV_WIDTH = 3
D_FF = 4 * D_MODEL
EPS = 1e-6

LANES = 128
SUBLANES = 8

TM_PROMPT = 512
TM_MLP = 512
TF_MLP = 1024
RET_CHUNK = 256
RET_SEQS_PER_STEP = 8
PAGES_PER_STEP = 8
MOBA_TILES_PER_STEP = 2

_HI = lax.Precision.HIGHEST


def _params(*sem):
    return pltpu.CompilerParams(dimension_semantics=sem)


def _row_tile(m, want):
    t = min(m, want)
    assert m % t == 0, (m, t)
    return t


def _rms(x, gamma):
    ms = jnp.mean(x * x, axis=-1, keepdims=True)
    return x * lax.rsqrt(ms + EPS) * gamma


def _dot(a, b, precision=None):
    return jnp.dot(a, b, preferred_element_type=F32, precision=precision)


def _dot_nt(a, b):
    return lax.dot_general(a, b, (((1,), (1,)), ((), ())), preferred_element_type=F32)


def _dot_tn(a, b):
    return lax.dot_general(a, b, (((0,), (0,)), ((), ())), preferred_element_type=F32)


def _iota(shape, dim):
    return lax.broadcasted_iota(jnp.int32, shape, dim)


def _norm_proj_kernel(x_ref, g_ref, w_ref, *out_refs, widths):
    h = _rms(x_ref[...], g_ref[...]).astype(BF16)
    off = 0
    for idx, wd in enumerate(widths):
        out_refs[idx][...] = _dot(h, w_ref[:, off:off + wd]).astype(out_refs[idx].dtype)
        off += wd


def _norm_proj(x, gamma, w, widths, dtypes, *, tm):
    m, d = x.shape
    tm = _row_tile(m, tm)
    n = sum(widths)
    return pl.pallas_call(
        functools.partial(_norm_proj_kernel, widths=tuple(widths)),
        grid=(m // tm,),
        in_specs=[pl.BlockSpec((tm, d), lambda i: (i, 0)),
                  pl.BlockSpec((1, d), lambda i: (0, 0)),
                  pl.BlockSpec((d, n), lambda i: (0, 0))],
        out_specs=[pl.BlockSpec((tm, wd), lambda i: (i, 0)) for wd in widths],
        out_shape=[jax.ShapeDtypeStruct((m, wd), dt) for wd, dt in zip(widths, dtypes)],
        compiler_params=_params("arbitrary"),
        name="norm_proj",
    )(x, gamma, w)


def _moba_proj_t_kernel(x_ref, g_ref, wt_ref, wk_ref, qt_ref, k_ref, kt_ref, vt_ref, km_ref):
    d = D_MODEL
    h = _rms(x_ref[...], g_ref[...]).astype(BF16)
    qt_ref[0] = _dot_nt(wt_ref[0:d, :], h)
    kt_ref[0] = _dot_nt(wt_ref[d:2 * d, :], h)
    vt_ref[0] = _dot_nt(wt_ref[2 * d:3 * d, :], h)
    k = _dot(h, wk_ref[...])
    k_ref[...] = k.astype(BF16)
    for r in range(k.shape[0] // MOBA_BLOCK):
        km_ref[r] = jnp.mean(k[r * MOBA_BLOCK:(r + 1) * MOBA_BLOCK], axis=0, keepdims=True)


def _moba_proj_t(x, gamma, w_qkv, *, batch, tm):
    m, d = x.shape
    s_len = m // batch
    tm = _row_tile(s_len, tm)
    assert tm % MOBA_BLOCK == 0
    per_seq = s_len // tm
    w_t = w_qkv.T.astype(BF16)
    w_k = w_qkv[:, d:2 * d].astype(BF16)
    feat = lambda i: (i // per_seq, 0, i % per_seq)
    t_shape = jax.ShapeDtypeStruct((batch, d, s_len), F32)
    return pl.pallas_call(
        _moba_proj_t_kernel,
        grid=(m // tm,),
        in_specs=[pl.BlockSpec((tm, d), lambda i: (i, 0)),
                  pl.BlockSpec((1, d), lambda i: (0, 0)),
                  pl.BlockSpec((3 * d, d), lambda i: (0, 0)),
                  pl.BlockSpec((d, d), lambda i: (0, 0))],
        out_specs=[pl.BlockSpec((1, d, tm), feat),
                   pl.BlockSpec((tm, d), lambda i: (i, 0)),
                   pl.BlockSpec((1, d, tm), feat),
                   pl.BlockSpec((1, d, tm), feat),
                   pl.BlockSpec((tm // MOBA_BLOCK, 1, d), lambda i: (i, 0, 0))],
        out_shape=[t_shape, jax.ShapeDtypeStruct((m, d), BF16), t_shape, t_shape,
                   jax.ShapeDtypeStruct((m // MOBA_BLOCK, 1, d), F32)],
        compiler_params=_params("arbitrary"),
        name="moba_proj_t",
    )(x, gamma, w_t, w_k)


def _ret_proj_kernel(x_ref, g_ref, w_ref, cos_ref, sin_ref, q_ref, k_ref, v_ref, gate_ref):
    h = _rms(x_ref[...], g_ref[...]).astype(BF16)
    hk = RET_HEADS * RET_DK
    hv = RET_HEADS * RET_DV
    even = (_iota((1, LANES), 1) % 2) == 0
    for off, scale, out in ((0, 1.0, q_ref), (hk, RET_DK ** -0.5, k_ref)):
        p = _dot(h, w_ref[:, off:off + hk])
        for c in range(hk // LANES):
            x = p[:, c * LANES:(c + 1) * LANES]
            swapped = jnp.where(even, pltpu.roll(x, LANES - 1, 1), pltpu.roll(x, 1, 1))
            tab = pl.ds((c * LANES) % RET_DK, LANES)
            rot = x * cos_ref[:, tab] + swapped * sin_ref[:, tab]
            out[:, c * LANES:(c + 1) * LANES] = (rot * scale).astype(out.dtype)
    v_ref[...] = _dot(h, w_ref[:, 2 * hk:2 * hk + hv]).astype(v_ref.dtype)
    gate_ref[...] = _dot(h, w_ref[:, 2 * hk + hv:])


def _ret_proj(x, gamma, w, cos, sin, *, tm, qv_dtype):
    m, d = x.shape
    tm = _row_tile(m, tm)
    hk = RET_HEADS * RET_DK
    hv = RET_HEADS * RET_DV
    n_tab = cos.shape[0] // tm
    assert cos.shape[0] % tm == 0
    row = lambda i: (i, 0)
    return pl.pallas_call(
        _ret_proj_kernel,
        grid=(m // tm,),
        in_specs=[pl.BlockSpec((tm, d), row),
                  pl.BlockSpec((1, d), lambda i: (0, 0)),
                  pl.BlockSpec((d, 2 * hk + 2 * hv), lambda i: (0, 0)),
                  pl.BlockSpec((tm, RET_DK), lambda i: (i % n_tab, 0)),
                  pl.BlockSpec((tm, RET_DK), lambda i: (i % n_tab, 0))],
        out_specs=[pl.BlockSpec((tm, hk), row), pl.BlockSpec((tm, hk), row),
                   pl.BlockSpec((tm, hv), row), pl.BlockSpec((tm, hv), row)],
        out_shape=[jax.ShapeDtypeStruct((m, hk), qv_dtype), jax.ShapeDtypeStruct((m, hk), F32),
                   jax.ShapeDtypeStruct((m, hv), qv_dtype), jax.ShapeDtypeStruct((m, hv), F32)],
        compiler_params=_params("arbitrary"),
        name="ret_proj",
    )(x, gamma, w, cos, sin)


def _rope_tables(pos):
    inv = ROPE_BASE ** (-jnp.arange(0, RET_DK, 2, dtype=F32) / RET_DK)
    ang = pos.astype(F32)[:, None] * inv[None, :]
    cos = jnp.repeat(jnp.cos(ang), 2, axis=1)
    sin = jnp.sin(ang)
    return cos, jnp.stack([-sin, sin], axis=-1).reshape(cos.shape)


def _ret_core_kernel(*refs, nb, chunk, n_chunks, has_state, layer, fill_layers):
    refs = list(refs)
    q_ref, k_ref, v_ref, g_ref, dm_ref, qd_ref, kd_ref = refs[:7]
    st0_ref = refs[7] if has_state else None
    o_ref, stout_ref, st_scr = refs[-3:]
    c = pl.program_id(2)

    @pl.when(c == 0)
    def _():
        for s in range(nb):
            st_scr[s] = st0_ref[s, 0] if has_state else jnp.zeros((RET_DK, RET_DV), F32)

    dm = dm_ref[0]
    qd = qd_ref[0]
    kd = kd_ref[0]
    qd_v = jnp.concatenate([qd] * (RET_DV // LANES), axis=1)
    kd_k = jnp.concatenate([kd] * (RET_DK // LANES), axis=1)
    sdec = qd[chunk - 1:chunk, 0:1]
    for s in range(nb):
        q = q_ref[s].astype(BF16)
        kf = k_ref[s]
        v = v_ref[s].astype(BF16)
        st = st_scr[s]
        sc = _dot_nt(q, kf.astype(BF16)) * dm
        inner = _dot(sc.astype(BF16), v)
        cross = _dot(q, st.astype(BF16)) * qd_v
        o = inner + cross
        kv = _dot_tn((kf * kd_k).astype(BF16), v)
        st_new = sdec * st + kv
        st_scr[s] = st_new
        on = o * lax.rsqrt(jnp.mean(o * o, axis=-1, keepdims=True) + EPS)
        gg = g_ref[s]
        o_ref[s] = (on * (gg * jax.nn.sigmoid(gg))).astype(o_ref.dtype)

        @pl.when(c == n_chunks - 1)
        def _():
            if fill_layers:
                for jj in range(fill_layers):
                    stout_ref[jj, s, 0] = st_new if jj == layer else jnp.zeros_like(st_new)
            else:
                stout_ref[s, 0] = st_new


def _ret_core(q, k, v, g, tabs, *, batch, chunk, nb, state_all=None, layer=0, out_all=None):
    hk = RET_HEADS * RET_DK
    hv = RET_HEADS * RET_DV
    s_len = q.shape[0] // batch
    n_chunks = s_len // chunk
    assert s_len % chunk == 0 and batch % nb == 0
    dm, qd, kd = tabs
    seq = lambda b, h, c: (b, c, h)
    tab = lambda b, h, c: (h, 0, 0)
    in_specs = [pl.BlockSpec((nb, chunk, RET_DK), seq), pl.BlockSpec((nb, chunk, RET_DK), seq),
                pl.BlockSpec((nb, chunk, RET_DV), seq), pl.BlockSpec((nb, chunk, RET_DV), seq),
                pl.BlockSpec((1, chunk, chunk), tab), pl.BlockSpec((1, chunk, LANES), tab),
                pl.BlockSpec((1, chunk, LANES), tab)]
    args = [q.reshape(batch, s_len, hk), k.reshape(batch, s_len, hk),
            v.reshape(batch, s_len, hv), g.reshape(batch, s_len, hv), dm, qd, kd]
    aliases = {}
    fill_layers = 0
    if state_all is None:
        st_shape = (batch, RET_HEADS, RET_DK, RET_DV)
        st_spec = pl.BlockSpec((nb, 1, RET_DK, RET_DV), lambda b, h, c: (b, h, 0, 0))
    else:
        n_layers = state_all.shape[0]
        st_shape = state_all.shape
        in_specs.append(pl.BlockSpec((None, nb, 1, RET_DK, RET_DV), lambda b, h, c: (layer, b, h, 0, 0)))
        args.append(state_all)
        if out_all is None:
            fill_layers = n_layers
            st_spec = pl.BlockSpec((n_layers, nb, 1, RET_DK, RET_DV), lambda b, h, c: (0, b, h, 0, 0))
        else:
            in_specs.append(pl.BlockSpec(memory_space=pl.ANY))
            args.append(out_all)
            aliases = {len(args) - 1: 1}
            st_spec = pl.BlockSpec((None, nb, 1, RET_DK, RET_DV), lambda b, h, c: (layer, b, h, 0, 0))
    kern = functools.partial(_ret_core_kernel, nb=nb, chunk=chunk, n_chunks=n_chunks,
                             has_state=state_all is not None, layer=layer, fill_layers=fill_layers)
    if aliases:
        inner = kern
        n_in = len(args)
        kern = lambda *refs: inner(*refs[:n_in - 1], *refs[n_in:])
    o, st = pl.pallas_call(
        kern,
        grid=(batch // nb, RET_HEADS, n_chunks),
        in_specs=in_specs,
        out_specs=[pl.BlockSpec((nb, chunk, RET_DV), seq), st_spec],
        out_shape=[jax.ShapeDtypeStruct((batch, s_len, hv), BF16),
                   jax.ShapeDtypeStruct(st_shape, F32)],
        scratch_shapes=[pltpu.VMEM((nb, RET_DK, RET_DV), F32)],
        input_output_aliases=aliases,
        compiler_params=_params("arbitrary", "arbitrary", "arbitrary"),
        name="ret_core",
    )(*args)
    return o.reshape(batch * s_len, hv), st


def _ret_prompt_kernel(q_ref, k_ref, v_ref, g_ref, dm_ref, qd_ref, kd_ref, o_ref, stout_ref, st_scr, *,
                       chunk, n_chunks):
    c = pl.program_id(1)
    dk, dv = RET_DK, RET_DV

    @pl.when(c == 0)
    def _():
        st_scr[...] = jnp.zeros((RET_HEADS, dk, dv), F32)

    for h in range(RET_HEADS):
        q = q_ref[0, :, h * dk:(h + 1) * dk]
        kf = k_ref[0, :, h * dk:(h + 1) * dk]
        v = v_ref[0, :, h * dv:(h + 1) * dv]
        st = st_scr[h]
        qd = qd_ref[h]
        kd_k = jnp.concatenate([kd_ref[h]] * (dk // LANES), axis=1)
        sc = _dot_nt(q, kf.astype(BF16))
        cross = _dot(q, st.astype(BF16))
        kv = _dot_tn((kf * kd_k).astype(BF16), v)
        inner = _dot((sc * dm_ref[h]).astype(BF16), v)
        o = inner + cross * jnp.concatenate([qd] * (dv // LANES), axis=1)
        st_scr[h] = qd[chunk - 1:chunk, 0:1] * st + kv
        on = o * lax.rsqrt(jnp.mean(o * o, axis=-1, keepdims=True) + EPS)
        gg = g_ref[0, :, h * dv:(h + 1) * dv]
        o_ref[0, :, h * dv:(h + 1) * dv] = (on * (gg * jax.nn.sigmoid(gg))).astype(o_ref.dtype)

    @pl.when(c == n_chunks - 1)
    def _():
        stout_ref[0] = st_scr[...]


def _ret_prompt(q, k, v, g, tabs, *, batch, chunk):
    hk = RET_HEADS * RET_DK
    hv = RET_HEADS * RET_DV
    s_len = q.shape[0] // batch
    n_chunks = s_len // chunk
    assert s_len % chunk == 0
    dm, qd, kd = tabs
    seq = lambda b, c: (b, c, 0)
    tab = lambda b, c: (0, 0, 0)
    o, st = pl.pallas_call(
        functools.partial(_ret_prompt_kernel, chunk=chunk, n_chunks=n_chunks),
        grid=(batch, n_chunks),
        in_specs=[pl.BlockSpec((1, chunk, hk), seq), pl.BlockSpec((1, chunk, hk), seq),
                  pl.BlockSpec((1, chunk, hv), seq), pl.BlockSpec((1, chunk, hv), seq),
                  pl.BlockSpec((RET_HEADS, chunk, chunk), tab), pl.BlockSpec((RET_HEADS, chunk, LANES), tab),
                  pl.BlockSpec((RET_HEADS, chunk, LANES), tab)],
        out_specs=[pl.BlockSpec((1, chunk, hv), seq),
                   pl.BlockSpec((1, RET_HEADS, RET_DK, RET_DV), lambda b, c: (b, 0, 0, 0))],
        out_shape=[jax.ShapeDtypeStruct((batch, s_len, hv), BF16),
                   jax.ShapeDtypeStruct((batch, RET_HEADS, RET_DK, RET_DV), F32)],
        scratch_shapes=[pltpu.VMEM((RET_HEADS, RET_DK, RET_DV), F32)],
        compiler_params=_params("arbitrary", "arbitrary"),
        name="ret_prompt",
    )(q.reshape(batch, s_len, hk), k.reshape(batch, s_len, hk), v.reshape(batch, s_len, hv),
      g.reshape(batch, s_len, hv), dm, qd, kd)
    return o.reshape(batch * s_len, hv), st


def _ret_tables(chunk):
    log_g = jnp.log(1.0 - 2.0 ** (-5.0 - jnp.arange(RET_HEADS, dtype=F32)))
    i = jnp.arange(chunk, dtype=F32)
    diff = i[:, None] - i[None, :]
    dm = jnp.where(diff >= 0, jnp.exp(log_g[:, None, None] * jnp.maximum(diff, 0.0)), 0.0)
    qd = jnp.exp(log_g[:, None] * (i[None, :] + 1.0))
    kd = jnp.exp(log_g[:, None] * (chunk - 1.0 - i[None, :]))
    rep = lambda t: jnp.broadcast_to(t[:, :, None], (RET_HEADS, chunk, LANES))
    return dm, rep(qd), rep(kd)


def _mlp_kernel(*refs, tf, has_proj, has_final):
    refs = list(refs)
    x_ref = refs.pop(0)
    a_ref, wo_ref = (refs.pop(0), refs.pop(0)) if has_proj else (None, None)
    g_ref, wu_ref, wd_ref = refs.pop(0), refs.pop(0), refs.pop(0)
    gf_ref = refs.pop(0) if has_final else None
    o_ref, = refs
    x = x_ref[...]
    if has_proj:
        x = x + _dot(a_ref[...].astype(BF16), wo_ref[...])
    h = _rms(x, g_ref[...]).astype(BF16)
    acc = None
    for f in range(D_FF // tf):
        a = jnp.maximum(_dot(h, wu_ref[:, f * tf:(f + 1) * tf]), 0.0)
        part = _dot((a * a).astype(BF16), wd_ref[f * tf:(f + 1) * tf, :])
        acc = part if acc is None else acc + part
    y = x + acc
    o_ref[...] = _rms(y, gf_ref[...]) if has_final else y


def _mlp(x, gamma, w_up, w_down, *, tm, tf, a=None, w_o=None, final_gamma=None):
    m, d = x.shape
    tm = _row_tile(m, tm)
    row = lambda i: (i, 0)
    const = lambda i: (0, 0)
    resident = pl.Buffered(1)
    in_specs = [pl.BlockSpec((tm, d), row)]
    args = [x]
    if a is not None:
        in_specs += [pl.BlockSpec((tm, a.shape[1]), row),
                     pl.BlockSpec(w_o.shape, const, pipeline_mode=resident)]
        args += [a, w_o]
    in_specs += [pl.BlockSpec((1, d), const),
                 pl.BlockSpec((d, D_FF), const, pipeline_mode=resident),
                 pl.BlockSpec((D_FF, d), const, pipeline_mode=resident)]
    args += [gamma, w_up, w_down]
    if final_gamma is not None:
        in_specs.append(pl.BlockSpec((1, d), const))
        args.append(final_gamma)
    return pl.pallas_call(
        functools.partial(_mlp_kernel, tf=tf, has_proj=a is not None, has_final=final_gamma is not None),
        grid=(m // tm,),
        in_specs=in_specs,
        out_specs=pl.BlockSpec((tm, d), row),
        out_shape=jax.ShapeDtypeStruct((m, d), F32),
        compiler_params=_params("arbitrary"),
        name="mlp",
    )(*args)


def _conv_kernel(*refs, tiles_per_seq, seq_in_tile):
    if seq_in_tile:
        x_ref, g_ref, win_ref, wc_ref, wout_ref, p1_ref, p2_ref, o_ref, u_ref = refs
    else:
        x_ref, g_ref, win_ref, wc_ref, wout_ref, o_ref, u_ref, carry_scr = refs
    d = D_MODEL
    x = x_ref[...]
    tm = x.shape[0]
    h = _rms(x, g_ref[...]).astype(BF16)
    bg = _dot(h, win_ref[:, 0:d])
    u = _dot(h, win_ref[:, d:2 * d]) * _dot(h, win_ref[:, 2 * d:3 * d])
    row = _iota((tm, 1), 0)
    r1 = pltpu.roll(u, 1, 0)
    r2 = pltpu.roll(u, 2, 0)
    if seq_in_tile:
        t = row % seq_in_tile
        um1 = jnp.where(t >= 1, r1, 0.0) + p1_ref[...]
        um2 = jnp.where(t >= 2, r2, 0.0) + p2_ref[...]
        u_ref[...] = u
    else:
        @pl.when((pl.program_id(0) % tiles_per_seq) == 0)
        def _():
            carry_scr[...] = jnp.zeros((SUBLANES, d), F32)

        carry = carry_scr[...]
        c1 = carry[SUBLANES - 1:SUBLANES]
        c2 = carry[SUBLANES - 2:SUBLANES - 1]
        um1 = jnp.where(row >= 1, r1, c1)
        um2 = jnp.where(row >= 2, r2, jnp.where(row == 1, c1, c2))
        tail = u[tm - SUBLANES:tm]
        carry_scr[...] = tail
        u_ref[...] = tail
    wc = wc_ref[...]
    y = wc[0:1] * um2 + wc[1:2] * um1 + wc[2:3] * u
    o_ref[...] = x + _dot((bg * y).astype(BF16), wout_ref[...])


def _conv_mixer(x, gamma, w_in, w_c, w_out, *, tm, seq_len, state=None):
    m, d = x.shape
    n_seq = m // seq_len
    wc_pad = jnp.zeros((SUBLANES, d), F32).at[:CONV_WIDTH].set(w_c)
    row = lambda i: (i, 0)
    const = lambda i: (0, 0)
    in_specs = [pl.BlockSpec((tm, d), row), pl.BlockSpec((1, d), const),
                pl.BlockSpec((d, 3 * d), const), pl.BlockSpec((SUBLANES, d), const),
                pl.BlockSpec((d, d), const)]
    args = [x, gamma, w_in, wc_pad, w_out]
    if state is not None:
        tm = _row_tile(m, tm)
        assert tm % seq_len == 0 and seq_len >= CONV_WIDTH - 1
        zeros = jnp.zeros((n_seq, seq_len, d), F32)
        p1 = zeros.at[:, 0].set(state[:, 1]).reshape(m, d)
        p2 = zeros.at[:, 0].set(state[:, 0]).at[:, 1].set(state[:, 1]).reshape(m, d)
        in_specs += [pl.BlockSpec((tm, d), row), pl.BlockSpec((tm, d), row)]
        args += [p1, p2]
        u_rows, u_blk, scratch = m, tm, []
        kern = functools.partial(_conv_kernel, tiles_per_seq=0, seq_in_tile=seq_len)
    else:
        tm = _row_tile(seq_len, tm)
        u_rows, u_blk = (m // tm) * SUBLANES, SUBLANES
        scratch = [pltpu.VMEM((SUBLANES, d), F32)]
        kern = functools.partial(_conv_kernel, tiles_per_seq=seq_len // tm, seq_in_tile=0)
    out, u = pl.pallas_call(
        kern,
        grid=(m // tm,),
        in_specs=in_specs,
        out_specs=[pl.BlockSpec((tm, d), row), pl.BlockSpec((u_blk, d), row)],
        out_shape=[jax.ShapeDtypeStruct((m, d), F32), jax.ShapeDtypeStruct((u_rows, d), F32)],
        scratch_shapes=scratch,
        compiler_params=_params("arbitrary"),
        name="conv_mixer",
    )(*args)
    u_last = u.reshape(n_seq, -1, d)[:, -(CONV_WIDTH - 1):]
    return out, u_last


def _topk_bias(gate, blk, valid, axis):
    n = gate.shape[axis]
    g = jnp.where(valid, gate, NEG_INF)
    sel = jnp.zeros(gate.shape, jnp.bool_)
    for _ in range(MOBA_TOPK):
        mx = jnp.max(g, axis=axis, keepdims=True)
        idx = jnp.min(jnp.where(g == mx, blk, n), axis=axis, keepdims=True)
        pick = blk == idx
        sel = sel | (pick & valid)
        g = jnp.where(pick, NEG_INF, g)
    return jnp.where(sel, 0.0, NEG_INF)


def _moba_prompt_kernel(qt_ref, k_ref, vt_ref, km_ref, o_ref, bias_scr, *, nblk, n_tiles):
    i = pl.program_id(2)
    bs = MOBA_BLOCK
    hd = MOBA_HD
    per_tile = LANES // hd
    heads = [(t, h2) for t in range(n_tiles) for h2 in range(per_tile)]
    feat_lane = _iota((1, LANES), 1) // hd
    feat_row = _iota((LANES, 1), 0) // hd
    blk = _iota((nblk, 1), 0)
    causal = _iota((bs, 1), 0) <= _iota((1, bs), 1)
    valid = blk < i

    def key_block(j):
        start = pl.multiple_of(j * bs, bs)
        return k_ref[pl.ds(start, bs), :], vt_ref[0, :, pl.ds(start, bs)].astype(BF16)

    def scores(kj, qbs):
        return [_dot(kj[:, t * LANES:(t + 1) * LANES], qb) for (t, _), qb in zip(heads, qbs)]

    def values(vtj, ps):
        return [_dot(vtj[n * hd:(n + 1) * hd], p.astype(BF16)) for n, p in enumerate(ps)]

    qbs = []
    for n, (t, h2) in enumerate(heads):
        qt = qt_ref[0, t * LANES:(t + 1) * LANES, :]
        km = km_ref[:, t * LANES:(t + 1) * LANES]
        gate = _dot(jnp.where(feat_lane == h2, km, 0.0), qt, precision=_HI)
        bias_scr[n] = _topk_bias(gate, blk, valid, 0)
        qbs.append(jnp.where(feat_row == h2, qt * (hd ** -0.5), 0.0).astype(BF16))

    k_own, vt_own = key_block(i)
    ss = [jnp.where(causal, s, NEG_INF) for s in scores(k_own, qbs)]
    ms = [jnp.max(s, axis=0, keepdims=True) for s in ss]
    ps = [jnp.exp(s - m) for s, m in zip(ss, ms)]
    ls = [jnp.sum(p, axis=0, keepdims=True) for p in ps]
    accs = values(vt_own, ps)

    def body(jj, carry):
        ms, ls, accs = carry
        blocks = [key_block(2 * jj + u) for u in range(2)]
        ss = [[s + bias_scr[n, pl.ds(2 * jj + u, 1), :] for n, s in enumerate(scores(blocks[u][0], qbs))]
              for u in range(2)]
        m_new = [jnp.maximum(m, jnp.maximum(jnp.max(s0, axis=0, keepdims=True), jnp.max(s1, axis=0, keepdims=True)))
                 for m, s0, s1 in zip(ms, ss[0], ss[1])]
        alphas = [jnp.exp(m - mn) for m, mn in zip(ms, m_new)]
        ps = [[jnp.exp(s - mn) for s, mn in zip(ss[u], m_new)] for u in range(2)]
        ls = [a * l + jnp.sum(p0, axis=0, keepdims=True) + jnp.sum(p1, axis=0, keepdims=True)
              for a, l, p0, p1 in zip(alphas, ls, ps[0], ps[1])]
        pvs = [values(blocks[u][1], ps[u]) for u in range(2)]
        accs = [a * acc + pv0 + pv1 for a, acc, pv0, pv1 in zip(alphas, accs, pvs[0], pvs[1])]
        return m_new, ls, accs

    ms, ls, accs = lax.fori_loop(0, (i + 1) // 2, body, (ms, ls, accs))
    ot = jnp.concatenate([acc / l for acc, l in zip(accs, ls)], axis=0)
    o_ref[...] = ot.T.astype(o_ref.dtype)


def _moba_prompt(qt, k, vt, kmean):
    batch, d, s_len = qt.shape
    nblk = s_len // MOBA_BLOCK
    assert s_len % MOBA_BLOCK == 0 and nblk % SUBLANES == 0
    n_tiles = MOBA_TILES_PER_STEP
    width = n_tiles * LANES
    km2 = kmean.reshape(batch * nblk, d)
    return pl.pallas_call(
        functools.partial(_moba_prompt_kernel, nblk=nblk, n_tiles=n_tiles),
        grid=(batch, d // width, nblk),
        in_specs=[pl.BlockSpec((1, width, MOBA_BLOCK), lambda b, h, i: (b, h, i)),
                  pl.BlockSpec((s_len, width), lambda b, h, i: (b, h)),
                  pl.BlockSpec((1, width, s_len), lambda b, h, i: (b, h, 0)),
                  pl.BlockSpec((nblk, width), lambda b, h, i: (b, h))],
        out_specs=pl.BlockSpec((MOBA_BLOCK, width), lambda b, h, i: (b * nblk + i, h)),
        out_shape=jax.ShapeDtypeStruct((batch * s_len, d), BF16),
        scratch_shapes=[pltpu.VMEM((width // MOBA_HD, nblk, MOBA_BLOCK), F32)],
        compiler_params=_params("arbitrary", "arbitrary", "arbitrary"),
        name="moba_prompt",
    )(qt, k, vt, km2)


def _moba_paged_kernel(pt_ref, q_ref, kn_ref, vn_ref, *refs, n_pages, page, t_new):
    pg = PAGES_PER_STEP
    k_refs = refs[0:pg]
    v_refs = refs[pg:2 * pg]
    o_ref = refs[2 * pg]
    qbd_scr, qbdf_scr, s_scr, p_scr, kmean_scr, acc_scr = refs[2 * pg + 1:]
    del pt_ref
    ph = pl.program_id(1)
    n = pl.program_id(2)
    n_grp = n_pages // pg
    ppb = MOBA_BLOCK // page
    n_blk = n_pages // ppb
    n_col = MOBA_HEADS * t_new
    d = D_MODEL
    blk_lane = _iota((1, LANES), 1)

    @pl.when((ph == 0) & (n == 0))
    def _():
        q = q_ref[0]
        qt = jnp.concatenate([q] * MOBA_HEADS, axis=0)
        bd = _iota((n_col, d), 0) // t_new == _iota((n_col, d), 1) // MOBA_HD
        qbdf = jnp.where(bd, qt, 0.0)
        qbdf_scr[...] = qbdf
        qbd_scr[...] = (qbdf * (MOBA_HD ** -0.5)).astype(BF16)
        kmean_scr[...] = jnp.zeros((d, LANES), F32)

    @pl.when(ph == 0)
    def _():
        qbd = qbd_scr[...]
        for b2 in range(pg // ppb):
            ksum = None
            for r in range(ppb):
                kp = k_refs[b2 * ppb + r][0]
                s_scr[n * pg + b2 * ppb + r] = _dot(qbd, kp.astype(BF16))
                ksum = kp if ksum is None else ksum + kp
            kmean = jnp.sum(ksum, axis=1, keepdims=True) * (1.0 / MOBA_BLOCK)
            lane_hit = blk_lane == n * (pg // ppb) + b2
            kmean_scr[...] = jnp.where(lane_hit, kmean, kmean_scr[...])

    @pl.when((ph == 1) & (n == 0))
    def _():
        gate = _dot(qbdf_scr[...], kmean_scr[...], precision=_HI)
        bias = _topk_bias(gate, blk_lane, blk_lane < n_blk, 1)
        qbd = qbd_scr[...]
        s_own = _dot_nt(qbd, kn_ref[0].astype(BF16))
        ok_own = _iota((n_col, t_new), 1) <= _iota((n_col, t_new), 0) % t_new
        s_own = jnp.where(ok_own, s_own, NEG_INF)
        m = jnp.max(s_own, axis=1, keepdims=True)
        for p in range(n_pages):
            b = p // ppb
            m = jnp.maximum(m, jnp.max(s_scr[p] + bias[:, b:b + 1], axis=1, keepdims=True))
        e_own = jnp.exp(s_own - m)
        l = jnp.sum(e_own, axis=1, keepdims=True)
        for p in range(n_pages):
            b = p // ppb
            l = l + jnp.sum(jnp.exp(s_scr[p] + bias[:, b:b + 1] - m), axis=1, keepdims=True)
        inv = 1.0 / l
        for p in range(n_pages):
            b = p // ppb
            p_scr[p] = (jnp.exp(s_scr[p] + bias[:, b:b + 1] - m) * inv).astype(BF16)
        acc_scr[...] = _dot((e_own * inv).astype(BF16), vn_ref[0].astype(BF16))

    @pl.when(ph == 1)
    def _():
        acc = acc_scr[...]
        for r in range(pg):
            acc = acc + _dot_nt(p_scr[n * pg + r], v_refs[r][0].astype(BF16))
        acc_scr[...] = acc

        @pl.when(n == n_grp - 1)
        def _():
            bd = _iota((n_col, d), 0) // t_new == _iota((n_col, d), 1) // MOBA_HD
            a = jnp.where(bd, acc, 0.0)
            out = a[0:t_new]
            for hd in range(1, MOBA_HEADS):
                out = out + a[hd * t_new:(hd + 1) * t_new]
            o_ref[0] = out


def _moba_paged(q, k_new, v_new, cache_kt, cache_vt, page_table):
    db, t_new, d = q.shape
    n_pages = page_table.shape[1]
    page = cache_kt.shape[2]
    pg = PAGES_PER_STEP
    past = n_pages * page
    ppb = MOBA_BLOCK // page
    assert past % MOBA_BLOCK == 0 and MOBA_BLOCK % page == 0 and n_pages % pg == 0 and pg % ppb == 0
    assert t_new <= MOBA_BLOCK and MOBA_TOPK <= past // MOBA_BLOCK <= LANES
    n_grp = n_pages // pg
    n_col = MOBA_HEADS * t_new
    pt = page_table.reshape(-1).astype(jnp.int32)

    def k_map(r):
        return lambda b, ph, n, pt_ref: (
            pt_ref[b * n_pages + jnp.where(ph == 0, n, n_grp - 1) * pg + r], 0, 0)

    def v_map(r):
        return lambda b, ph, n, pt_ref: (
            pt_ref[b * n_pages + jnp.where(ph == 0, 0, n) * pg + r], 0, 0)

    tok = pl.BlockSpec((1, t_new, d), lambda b, ph, n, pt_ref: (b, 0, 0))
    in_specs = [tok, tok, tok]
    in_specs += [pl.BlockSpec((1, d, page), k_map(r)) for r in range(pg)]
    in_specs += [pl.BlockSpec((1, d, page), v_map(r)) for r in range(pg)]
    grid_spec = pltpu.PrefetchScalarGridSpec(
        num_scalar_prefetch=1,
        grid=(db, 2, n_grp),
        in_specs=in_specs,
        out_specs=pl.BlockSpec((1, t_new, d), lambda b, ph, n, pt_ref: (b, 0, 0)),
        scratch_shapes=[pltpu.VMEM((n_col, d), BF16), pltpu.VMEM((n_col, d), F32),
                        pltpu.VMEM((n_pages, n_col, page), F32), pltpu.VMEM((n_pages, n_col, page), BF16),
                        pltpu.VMEM((d, LANES), F32), pltpu.VMEM((n_col, d), F32)],
    )
    return pl.pallas_call(
        functools.partial(_moba_paged_kernel, n_pages=n_pages, page=page, t_new=t_new),
        grid_spec=grid_spec,
        out_shape=jax.ShapeDtypeStruct((db, t_new, d), F32),
        compiler_params=_params("arbitrary", "arbitrary", "arbitrary"),
        name="moba_paged",
    )(pt, q, k_new, v_new, *([cache_kt] * pg), *([cache_vt] * pg))


def _feature_major_pages(cache):
    n_phys, page, n_h, hd = cache.shape
    return jnp.transpose(cache, (0, 2, 3, 1)).reshape(n_phys, n_h * hd, page)


def _token_rows(xt, batch):
    _, _, s_len = xt.shape
    return jnp.transpose(xt.reshape(batch, MOBA_HEADS, MOBA_HD, s_len), (0, 3, 1, 2))


def kernel(x_prompt, x_sample, state_ret, cache_k, cache_v, state_conv, page_table, norm_mix, norm_ff, norm_final, ret_w_qkvg, ret_w_o, moba_w_qkv, moba_w_o, conv_w_in, conv_w, conv_w_out, mlp_w_up, mlp_w_down):
    b, s, d = x_prompt.shape
    db, t, _ = x_sample.shape
    depth = norm_mix.shape[0]
    n_pages = page_table.shape[1]
    page = cache_k.shape[2]
    past_len = n_pages * page

    xp = x_prompt.reshape(b * s, d)
    xs = x_sample.reshape(db * t, d)
    tm_p = TM_PROMPT
    tm_s = db * t

    cos_p, sin_p = _rope_tables(jnp.arange(s, dtype=jnp.int32))
    cos_s, sin_s = _rope_tables(past_len + jnp.arange(t, dtype=jnp.int32))
    cos_s = jnp.tile(cos_s, (tm_s // t, 1))
    sin_s = jnp.tile(sin_s, (tm_s // t, 1))
    tabs_p = _ret_tables(RET_CHUNK)
    tabs_s = _ret_tables(t)

    ret_p, ret_s_all, kp_l, vp_l, ks_l, vs_l, conv_p, conv_s = [], None, [], [], [], [], [], []
    for i in range(depth):
        kind = i % N_MIXERS
        j = i // N_MIXERS
        g_mix = norm_mix[i].reshape(1, d)
        g_ff = norm_ff[i].reshape(1, d)
        if kind == 0:
            w = ret_w_qkvg[j].astype(BF16)
            w_o = ret_w_o[j].astype(BF16)
            q, k, v, g = _ret_proj(xp, g_mix, w, cos_p, sin_p, tm=tm_p, qv_dtype=BF16)
            op, st_p = _ret_prompt(q, k, v, g, tabs_p, batch=b, chunk=RET_CHUNK)
            q, k, v, g = _ret_proj(xs, g_mix, w, cos_s, sin_s, tm=tm_s, qv_dtype=F32)
            os_, ret_s_all = _ret_core(q, k, v, g, tabs_s, batch=db, chunk=t, nb=RET_SEQS_PER_STEP,
                                       state_all=state_ret, layer=j, out_all=ret_s_all)
            ret_p.append(st_p)
        elif kind == 1:
            w_o = moba_w_o[j].astype(BF16)
            qt, k, kt, vt, kmean = _moba_proj_t(xp, g_mix, moba_w_qkv[j], batch=b, tm=tm_p)
            op = _moba_prompt(qt, k, vt, kmean)
            kp_l.append(_token_rows(kt, b))
            vp_l.append(_token_rows(vt, b))
            q, k, v = _norm_proj(xs, g_mix, moba_w_qkv[j].astype(BF16), (d, d, d), (F32, F32, F32), tm=tm_s)
            os_ = _moba_paged(q.reshape(db, t, d), k.reshape(db, t, d), v.reshape(db, t, d),
                              _feature_major_pages(cache_k[j]), _feature_major_pages(cache_v[j]),
                              page_table).reshape(db * t, d)
            ks_l.append(k.reshape(db, t, MOBA_HEADS, MOBA_HD))
            vs_l.append(v.reshape(db, t, MOBA_HEADS, MOBA_HD))
        else:
            w_in = conv_w_in[j].astype(BF16)
            w_out = conv_w_out[j].astype(BF16)
            xp, buf_p = _conv_mixer(xp, g_mix, w_in, conv_w[j], w_out, tm=tm_p, seq_len=s)
            xs, buf_s = _conv_mixer(xs, g_mix, w_in, conv_w[j], w_out, tm=tm_s, seq_len=t,
                                    state=state_conv[j])
            conv_p.append(buf_p)
            conv_s.append(buf_s)
            op = os_ = w_o = None
        w_up = mlp_w_up[i].astype(BF16)
        w_down = mlp_w_down[i].astype(BF16)
        g_fin = norm_final.reshape(1, d) if i == depth - 1 else None
        xp = _mlp(xp, g_ff, w_up, w_down, tm=TM_MLP, tf=TF_MLP, a=op, w_o=w_o, final_gamma=g_fin)
        xs = _mlp(xs, g_ff, w_up, w_down, tm=tm_s, tf=TF_MLP, a=os_, w_o=w_o, final_gamma=g_fin)
    y_prompt = xp.reshape(b, s, d)
    y_sample = xs.reshape(db, t, d)
    return (y_prompt, y_sample, jnp.stack(ret_p), ret_s_all, jnp.stack(kp_l), jnp.stack(vp_l),
            jnp.stack(ks_l), jnp.stack(vs_l), jnp.stack(conv_p), jnp.stack(conv_s))
```

```python
import functools

import jax
import jax.numpy as jnp
from jax import lax
from jax.experimental import pallas as pl
from jax.experimental.pallas import tpu as pltpu

F32 = jnp.float32
BF16 = jnp.bfloat16
NEG_INF = float("-inf")

D_MODEL = 1024
N_MIXERS = 3
RET_HEADS = 4
RET_DK = D_MODEL // RET_HEADS
RET_DV = 2 * RET_DK
ROPE_BASE = 10000.0
MOBA_HEADS = 16
MOBA_HD = D_MODEL // MOBA_HEADS
MOBA_BLOCK = 256
MOBA_TOPK = 3
CONV_WIDTH = 3
D_FF = 4 * D_MODEL
EPS = 1e-6

LANES = 128
SUBLANES = 8

TM_PROMPT = 512
TM_MLP = 512
TF_MLP = 1024
RET_CHUNK = 256
RET_SEQS_PER_STEP = 8
MOBA_TILES_PER_STEP = 2

_HI = lax.Precision.HIGHEST


def _params(*sem):
    return pltpu.CompilerParams(dimension_semantics=sem)


def _row_tile(m, want):
    t = min(m, want)
    assert m % t == 0, (m, t)
    return t


def _rms(x, gamma):
    ms = jnp.mean(x * x, axis=-1, keepdims=True)
    return x * lax.rsqrt(ms + EPS) * gamma


def _dot(a, b, precision=None):
    return jnp.dot(a, b, preferred_element_type=F32, precision=precision)


def _dot_nt(a, b):
    return lax.dot_general(a, b, (((1,), (1,)), ((), ())), preferred_element_type=F32)


def _dot_tn(a, b):
    return lax.dot_general(a, b, (((0,), (0,)), ((), ())), preferred_element_type=F32)


def _iota(shape, dim):
    return lax.broadcasted_iota(jnp.int32, shape, dim)


def _norm_proj_kernel(x_ref, g_ref, w_ref, *out_refs, widths):
    h = _rms(x_ref[...], g_ref[...]).astype(BF16)
    off = 0
    for idx, wd in enumerate(widths):
        out_refs[idx][...] = _dot(h, w_ref[:, off:off + wd]).astype(out_refs[idx].dtype)
        off += wd


def _norm_proj(x, gamma, w, widths, dtypes, *, tm):
    m, d = x.shape
    tm = _row_tile(m, tm)
    n = sum(widths)
    return pl.pallas_call(
        functools.partial(_norm_proj_kernel, widths=tuple(widths)),
        grid=(m // tm,),
        in_specs=[pl.BlockSpec((tm, d), lambda i: (i, 0)),
                  pl.BlockSpec((1, d), lambda i: (0, 0)),
                  pl.BlockSpec((d, n), lambda i: (0, 0))],
        out_specs=[pl.BlockSpec((tm, wd), lambda i: (i, 0)) for wd in widths],
        out_shape=[jax.ShapeDtypeStruct((m, wd), dt) for wd, dt in zip(widths, dtypes)],
        compiler_params=_params("arbitrary"),
        name="norm_proj",
    )(x, gamma, w)


def _moba_proj_t_kernel(x_ref, g_ref, wt_ref, wk_ref, qt_ref, k_ref, kt_ref, vt_ref, km_ref):
    d = D_MODEL
    h = _rms(x_ref[...], g_ref[...]).astype(BF16)
    qt_ref[0] = _dot_nt(wt_ref[0:d, :], h)
    kt_ref[0] = _dot_nt(wt_ref[d:2 * d, :], h)
    vt_ref[0] = _dot_nt(wt_ref[2 * d:3 * d, :], h)
    k = _dot(h, wk_ref[...])
    k_ref[...] = k.astype(BF16)
    for r in range(k.shape[0] // MOBA_BLOCK):
        km_ref[r] = jnp.mean(k[r * MOBA_BLOCK:(r + 1) * MOBA_BLOCK], axis=0, keepdims=True)


def _moba_proj_t(x, gamma, w_qkv, *, batch, tm):
    m, d = x.shape
    s_len = m // batch
    tm = _row_tile(s_len, tm)
    assert tm % MOBA_BLOCK == 0
    per_seq = s_len // tm
    w_t = w_qkv.T.astype(BF16)
    w_k = w_qkv[:, d:2 * d].astype(BF16)
    feat = lambda i: (i // per_seq, 0, i % per_seq)
    t_shape = jax.ShapeDtypeStruct((batch, d, s_len), F32)
    return pl.pallas_call(
        _moba_proj_t_kernel,
        grid=(m // tm,),
        in_specs=[pl.BlockSpec((tm, d), lambda i: (i, 0)),
                  pl.BlockSpec((1, d), lambda i: (0, 0)),
                  pl.BlockSpec((3 * d, d), lambda i: (0, 0)),
                  pl.BlockSpec((d, d), lambda i: (0, 0))],
        out_specs=[pl.BlockSpec((1, d, tm), feat),
                   pl.BlockSpec((tm, d), lambda i: (i, 0)),
                   pl.BlockSpec((1, d, tm), feat),
                   pl.BlockSpec((1, d, tm), feat),
                   pl.BlockSpec((tm // MOBA_BLOCK, 1, d), lambda i: (i, 0, 0))],
        out_shape=[t_shape, jax.ShapeDtypeStruct((m, d), BF16), t_shape, t_shape,
                   jax.ShapeDtypeStruct((m // MOBA_BLOCK, 1, d), F32)],
        compiler_params=_params("arbitrary"),
        name="moba_proj_t",
    )(x, gamma, w_t, w_k)


def _ret_proj_kernel(x_ref, g_ref, w_ref, cos_ref, sin_ref, q_ref, k_ref, v_ref, gate_ref):
    h = _rms(x_ref[...], g_ref[...]).astype(BF16)
    hk = RET_HEADS * RET_DK
    hv = RET_HEADS * RET_DV
    even = (_iota((1, LANES), 1) % 2) == 0
    for off, scale, out in ((0, 1.0, q_ref), (hk, RET_DK ** -0.5, k_ref)):
        p = _dot(h, w_ref[:, off:off + hk])
        for c in range(hk // LANES):
            x = p[:, c * LANES:(c + 1) * LANES]
            swapped = jnp.where(even, pltpu.roll(x, LANES - 1, 1), pltpu.roll(x, 1, 1))
            tab = pl.ds((c * LANES) % RET_DK, LANES)
            rot = x * cos_ref[:, tab] + swapped * sin_ref[:, tab]
            out[:, c * LANES:(c + 1) * LANES] = (rot * scale).astype(out.dtype)
    v_ref[...] = _dot(h, w_ref[:, 2 * hk:2 * hk + hv]).astype(v_ref.dtype)
    gate_ref[...] = _dot(h, w_ref[:, 2 * hk + hv:])


def _ret_proj(x, gamma, w, cos, sin, *, tm, qv_dtype):
    m, d = x.shape
    tm = _row_tile(m, tm)
    hk = RET_HEADS * RET_DK
    hv = RET_HEADS * RET_DV
    n_tab = cos.shape[0] // tm
    assert cos.shape[0] % tm == 0
    row = lambda i: (i, 0)
    return pl.pallas_call(
        _ret_proj_kernel,
        grid=(m // tm,),
        in_specs=[pl.BlockSpec((tm, d), row),
                  pl.BlockSpec((1, d), lambda i: (0, 0)),
                  pl.BlockSpec((d, 2 * hk + 2 * hv), lambda i: (0, 0)),
                  pl.BlockSpec((tm, RET_DK), lambda i: (i % n_tab, 0)),
                  pl.BlockSpec((tm, RET_DK), lambda i: (i % n_tab, 0))],
        out_specs=[pl.BlockSpec((tm, hk), row), pl.BlockSpec((tm, hk), row),
                   pl.BlockSpec((tm, hv), row), pl.BlockSpec((tm, hv), row)],
        out_shape=[jax.ShapeDtypeStruct((m, hk), qv_dtype), jax.ShapeDtypeStruct((m, hk), F32),
                   jax.ShapeDtypeStruct((m, hv), qv_dtype), jax.ShapeDtypeStruct((m, hv), F32)],
        compiler_params=_params("arbitrary"),
        name="ret_proj",
    )(x, gamma, w, cos, sin)


def _rope_tables(pos):
    inv = ROPE_BASE ** (-jnp.arange(0, RET_DK, 2, dtype=F32) / RET_DK)
    ang = pos.astype(F32)[:, None] * inv[None, :]
    cos = jnp.repeat(jnp.cos(ang), 2, axis=1)
    sin = jnp.sin(ang)
    return cos, jnp.stack([-sin, sin], axis=-1).reshape(cos.shape)


def _ret_core_kernel(*refs, nb, chunk, n_chunks, has_state, layer, fill_layers):
    refs = list(refs)
    q_ref, k_ref, v_ref, g_ref, dm_ref, qd_ref, kd_ref = refs[:7]
    st0_ref = refs[7] if has_state else None
    o_ref, stout_ref, st_scr = refs[-3:]
    c = pl.program_id(2)

    @pl.when(c == 0)
    def _():
        for s in range(nb):
            st_scr[s] = st0_ref[s, 0] if has_state else jnp.zeros((RET_DK, RET_DV), F32)

    dm = dm_ref[0]
    qd = qd_ref[0]
    kd = kd_ref[0]
    qd_v = jnp.concatenate([qd] * (RET_DV // LANES), axis=1)
    kd_k = jnp.concatenate([kd] * (RET_DK // LANES), axis=1)
    sdec = qd[chunk - 1:chunk, 0:1]
    for s in range(nb):
        q = q_ref[s].astype(BF16)
        kf = k_ref[s]
        v = v_ref[s].astype(BF16)
        st = st_scr[s]
        sc = _dot_nt(q, kf.astype(BF16)) * dm
        inner = _dot(sc.astype(BF16), v)
        cross = _dot(q, st.astype(BF16)) * qd_v
        o = inner + cross
        kv = _dot_tn((kf * kd_k).astype(BF16), v)
        st_new = sdec * st + kv
        st_scr[s] = st_new
        on = o * lax.rsqrt(jnp.mean(o * o, axis=-1, keepdims=True) + EPS)
        gg = g_ref[s]
        o_ref[s] = (on * (gg * jax.nn.sigmoid(gg))).astype(o_ref.dtype)

        @pl.when(c == n_chunks - 1)
        def _():
            if fill_layers:
                for jj in range(fill_layers):
                    stout_ref[jj, s, 0] = st_new if jj == layer else jnp.zeros_like(st_new)
            else:
                stout_ref[s, 0] = st_new


def _ret_core(q, k, v, g, tabs, *, batch, chunk, nb, state_all=None, layer=0, out_all=None):
    hk = RET_HEADS * RET_DK
    hv = RET_HEADS * RET_DV
    s_len = q.shape[0] // batch
    n_chunks = s_len // chunk
    assert s_len % chunk == 0 and batch % nb == 0
    dm, qd, kd = tabs
    seq = lambda b, h, c: (b, c, h)
    tab = lambda b, h, c: (h, 0, 0)
    in_specs = [pl.BlockSpec((nb, chunk, RET_DK), seq), pl.BlockSpec((nb, chunk, RET_DK), seq),
                pl.BlockSpec((nb, chunk, RET_DV), seq), pl.BlockSpec((nb, chunk, RET_DV), seq),
                pl.BlockSpec((1, chunk, chunk), tab), pl.BlockSpec((1, chunk, LANES), tab),
                pl.BlockSpec((1, chunk, LANES), tab)]
    args = [q.reshape(batch, s_len, hk), k.reshape(batch, s_len, hk),
            v.reshape(batch, s_len, hv), g.reshape(batch, s_len, hv), dm, qd, kd]
    aliases = {}
    fill_layers = 0
    if state_all is None:
        st_shape = (batch, RET_HEADS, RET_DK, RET_DV)
        st_spec = pl.BlockSpec((nb, 1, RET_DK, RET_DV), lambda b, h, c: (b, h, 0, 0))
    else:
        n_layers = state_all.shape[0]
        st_shape = state_all.shape
        in_specs.append(pl.BlockSpec((None, nb, 1, RET_DK, RET_DV), lambda b, h, c: (layer, b, h, 0, 0)))
        args.append(state_all)
        if out_all is None:
            fill_layers = n_layers
            st_spec = pl.BlockSpec((n_layers, nb, 1, RET_DK, RET_DV), lambda b, h, c: (0, b, h, 0, 0))
        else:
            in_specs.append(pl.BlockSpec(memory_space=pl.ANY))
            args.append(out_all)
            aliases = {len(args) - 1: 1}
            st_spec = pl.BlockSpec((None, nb, 1, RET_DK, RET_DV), lambda b, h, c: (layer, b, h, 0, 0))
    kern = functools.partial(_ret_core_kernel, nb=nb, chunk=chunk, n_chunks=n_chunks,
                             has_state=state_all is not None, layer=layer, fill_layers=fill_layers)
    if aliases:
        inner = kern
        n_in = len(args)
        kern = lambda *refs: inner(*refs[:n_in - 1], *refs[n_in:])
    o, st = pl.pallas_call(
        kern,
        grid=(batch // nb, RET_HEADS, n_chunks),
        in_specs=in_specs,
        out_specs=[pl.BlockSpec((nb, chunk, RET_DV), seq), st_spec],
        out_shape=[jax.ShapeDtypeStruct((batch, s_len, hv), BF16),
                   jax.ShapeDtypeStruct(st_shape, F32)],
        scratch_shapes=[pltpu.VMEM((nb, RET_DK, RET_DV), F32)],
        input_output_aliases=aliases,
        compiler_params=_params("arbitrary", "arbitrary", "arbitrary"),
        name="ret_core",
    )(*args)
    return o.reshape(batch * s_len, hv), st


def _ret_prompt_kernel(q_ref, k_ref, v_ref, g_ref, dm_ref, qd_ref, kd_ref, o_ref, stout_ref, st_scr, *,
                       chunk, n_chunks):
    c = pl.program_id(1)
    dk, dv = RET_DK, RET_DV

    @pl.when(c == 0)
    def _():
        st_scr[...] = jnp.zeros((RET_HEADS, dk, dv), F32)

    for h in range(RET_HEADS):
        q = q_ref[0, :, h * dk:(h + 1) * dk]
        kf = k_ref[0, :, h * dk:(h + 1) * dk]
        v = v_ref[0, :, h * dv:(h + 1) * dv]
        st = st_scr[h]
        qd = qd_ref[h]
        kd_k = jnp.concatenate([kd_ref[h]] * (dk // LANES), axis=1)
        sc = _dot_nt(q, kf.astype(BF16))
        cross = _dot(q, st.astype(BF16))
        kv = _dot_tn((kf * kd_k).astype(BF16), v)
        inner = _dot((sc * dm_ref[h]).astype(BF16), v)
        o = inner + cross * jnp.concatenate([qd] * (dv // LANES), axis=1)
        st_scr[h] = qd[chunk - 1:chunk, 0:1] * st + kv
        on = o * lax.rsqrt(jnp.mean(o * o, axis=-1, keepdims=True) + EPS)
        gg = g_ref[0, :, h * dv:(h + 1) * dv]
        o_ref[0, :, h * dv:(h + 1) * dv] = (on * (gg * jax.nn.sigmoid(gg))).astype(o_ref.dtype)

    @pl.when(c == n_chunks - 1)
    def _():
        stout_ref[0] = st_scr[...]


def _ret_prompt(q, k, v, g, tabs, *, batch, chunk):
    hk = RET_HEADS * RET_DK
    hv = RET_HEADS * RET_DV
    s_len = q.shape[0] // batch
    n_chunks = s_len // chunk
    assert s_len % chunk == 0
    dm, qd, kd = tabs
    seq = lambda b, c: (b, c, 0)
    tab = lambda b, c: (0, 0, 0)
    o, st = pl.pallas_call(
        functools.partial(_ret_prompt_kernel, chunk=chunk, n_chunks=n_chunks),
        grid=(batch, n_chunks),
        in_specs=[pl.BlockSpec((1, chunk, hk), seq), pl.BlockSpec((1, chunk, hk), seq),
                  pl.BlockSpec((1, chunk, hv), seq), pl.BlockSpec((1, chunk, hv), seq),
                  pl.BlockSpec((RET_HEADS, chunk, chunk), tab), pl.BlockSpec((RET_HEADS, chunk, LANES), tab),
                  pl.BlockSpec((RET_HEADS, chunk, LANES), tab)],
        out_specs=[pl.BlockSpec((1, chunk, hv), seq),
                   pl.BlockSpec((1, RET_HEADS, RET_DK, RET_DV), lambda b, c: (b, 0, 0, 0))],
        out_shape=[jax.ShapeDtypeStruct((batch, s_len, hv), BF16),
                   jax.ShapeDtypeStruct((batch, RET_HEADS, RET_DK, RET_DV), F32)],
        scratch_shapes=[pltpu.VMEM((RET_HEADS, RET_DK, RET_DV), F32)],
        compiler_params=_params("arbitrary", "arbitrary"),
        name="ret_prompt",
    )(q.reshape(batch, s_len, hk), k.reshape(batch, s_len, hk), v.reshape(batch, s_len, hv),
      g.reshape(batch, s_len, hv), dm, qd, kd)
    return o.reshape(batch * s_len, hv), st


def _ret_tables(chunk):
    log_g = jnp.log(1.0 - 2.0 ** (-5.0 - jnp.arange(RET_HEADS, dtype=F32)))
    i = jnp.arange(chunk, dtype=F32)
    diff = i[:, None] - i[None, :]
    dm = jnp.where(diff >= 0, jnp.exp(log_g[:, None, None] * jnp.maximum(diff, 0.0)), 0.0)
    qd = jnp.exp(log_g[:, None] * (i[None, :] + 1.0))
    kd = jnp.exp(log_g[:, None] * (chunk - 1.0 - i[None, :]))
    rep = lambda t: jnp.broadcast_to(t[:, :, None], (RET_HEADS, chunk, LANES))
    return dm, rep(qd), rep(kd)


def _mlp_kernel(*refs, tf, has_proj, has_final):
    refs = list(refs)
    x_ref = refs.pop(0)
    a_ref, wo_ref = (refs.pop(0), refs.pop(0)) if has_proj else (None, None)
    g_ref, wu_ref, wd_ref = refs.pop(0), refs.pop(0), refs.pop(0)
    gf_ref = refs.pop(0) if has_final else None
    o_ref, = refs
    x = x_ref[...]
    if has_proj:
        x = x + _dot(a_ref[...].astype(BF16), wo_ref[...])
    h = _rms(x, g_ref[...]).astype(BF16)
    acc = None
    for f in range(D_FF // tf):
        a = jnp.maximum(_dot(h, wu_ref[:, f * tf:(f + 1) * tf]), 0.0)
        part = _dot((a * a).astype(BF16), wd_ref[f * tf:(f + 1) * tf, :])
        acc = part if acc is None else acc + part
    y = x + acc
    o_ref[...] = _rms(y, gf_ref[...]) if has_final else y


def _mlp(x, gamma, w_up, w_down, *, tm, tf, a=None, w_o=None, final_gamma=None):
    m, d = x.shape
    tm = _row_tile(m, tm)
    row = lambda i: (i, 0)
    const = lambda i: (0, 0)
    resident = pl.Buffered(1)
    in_specs = [pl.BlockSpec((tm, d), row)]
    args = [x]
    if a is not None:
        in_specs += [pl.BlockSpec((tm, a.shape[1]), row),
                     pl.BlockSpec(w_o.shape, const, pipeline_mode=resident)]
        args += [a, w_o]
    in_specs += [pl.BlockSpec((1, d), const),
                 pl.BlockSpec((d, D_FF), const, pipeline_mode=resident),
                 pl.BlockSpec((D_FF, d), const, pipeline_mode=resident)]
    args += [gamma, w_up, w_down]
    if final_gamma is not None:
        in_specs.append(pl.BlockSpec((1, d), const))
        args.append(final_gamma)
    return pl.pallas_call(
        functools.partial(_mlp_kernel, tf=tf, has_proj=a is not None, has_final=final_gamma is not None),
        grid=(m // tm,),
        in_specs=in_specs,
        out_specs=pl.BlockSpec((tm, d), row),
        out_shape=jax.ShapeDtypeStruct((m, d), F32),
        compiler_params=_params("arbitrary"),
        name="mlp",
    )(*args)


def _conv_kernel(*refs, tiles_per_seq, seq_in_tile):
    if seq_in_tile:
        x_ref, g_ref, win_ref, wc_ref, wout_ref, p1_ref, p2_ref, o_ref, u_ref = refs
    else:
        x_ref, g_ref, win_ref, wc_ref, wout_ref, o_ref, u_ref, carry_scr = refs
    d = D_MODEL
    x = x_ref[...]
    tm = x.shape[0]
    h = _rms(x, g_ref[...]).astype(BF16)
    bg = _dot(h, win_ref[:, 0:d])
    u = _dot(h, win_ref[:, d:2 * d]) * _dot(h, win_ref[:, 2 * d:3 * d])
    row = _iota((tm, 1), 0)
    r1 = pltpu.roll(u, 1, 0)
    r2 = pltpu.roll(u, 2, 0)
    if seq_in_tile:
        t = row % seq_in_tile
        um1 = jnp.where(t >= 1, r1, 0.0) + p1_ref[...]
        um2 = jnp.where(t >= 2, r2, 0.0) + p2_ref[...]
        u_ref[...] = u
    else:
        @pl.when((pl.program_id(0) % tiles_per_seq) == 0)
        def _():
            carry_scr[...] = jnp.zeros((SUBLANES, d), F32)

        carry = carry_scr[...]
        c1 = carry[SUBLANES - 1:SUBLANES]
        c2 = carry[SUBLANES - 2:SUBLANES - 1]
        um1 = jnp.where(row >= 1, r1, c1)
        um2 = jnp.where(row >= 2, r2, jnp.where(row == 1, c1, c2))
        tail = u[tm - SUBLANES:tm]
        carry_scr[...] = tail
        u_ref[...] = tail
    wc = wc_ref[...]
    y = wc[0:1] * um2 + wc[1:2] * um1 + wc[2:3] * u
    o_ref[...] = x + _dot((bg * y).astype(BF16), wout_ref[...])


def _conv_mixer(x, gamma, w_in, w_c, w_out, *, tm, seq_len, state=None):
    m, d = x.shape
    n_seq = m // seq_len
    wc_pad = jnp.zeros((SUBLANES, d), F32).at[:CONV_WIDTH].set(w_c)
    row = lambda i: (i, 0)
    const = lambda i: (0, 0)
    in_specs = [pl.BlockSpec((tm, d), row), pl.BlockSpec((1, d), const),
                pl.BlockSpec((d, 3 * d), const), pl.BlockSpec((SUBLANES, d), const),
                pl.BlockSpec((d, d), const)]
    args = [x, gamma, w_in, wc_pad, w_out]
    if state is not None:
        tm = _row_tile(m, tm)
        assert tm % seq_len == 0 and seq_len >= CONV_WIDTH - 1
        zeros = jnp.zeros((n_seq, seq_len, d), F32)
        p1 = zeros.at[:, 0].set(state[:, 1]).reshape(m, d)
        p2 = zeros.at[:, 0].set(state[:, 0]).at[:, 1].set(state[:, 1]).reshape(m, d)
        in_specs += [pl.BlockSpec((tm, d), row), pl.BlockSpec((tm, d), row)]
        args += [p1, p2]
        u_rows, u_blk, scratch = m, tm, []
        kern = functools.partial(_conv_kernel, tiles_per_seq=0, seq_in_tile=seq_len)
    else:
        tm = _row_tile(seq_len, tm)
        u_rows, u_blk = (m // tm) * SUBLANES, SUBLANES
        scratch = [pltpu.VMEM((SUBLANES, d), F32)]
        kern = functools.partial(_conv_kernel, tiles_per_seq=seq_len // tm, seq_in_tile=0)
    out, u = pl.pallas_call(
        kern,
        grid=(m // tm,),
        in_specs=in_specs,
        out_specs=[pl.BlockSpec((tm, d), row), pl.BlockSpec((u_blk, d), row)],
        out_shape=[jax.ShapeDtypeStruct((m, d), F32), jax.ShapeDtypeStruct((u_rows, d), F32)],
        scratch_shapes=scratch,
        compiler_params=_params("arbitrary"),
        name="conv_mixer",
    )(*args)
    u_last = u.reshape(n_seq, -1, d)[:, -(CONV_WIDTH - 1):]
    return out, u_last


def _topk_bias(gate, blk, valid, axis):
    n = gate.shape[axis]
    g = jnp.where(valid, gate, NEG_INF)
    sel = jnp.zeros(gate.shape, jnp.bool_)
    for _ in range(MOBA_TOPK):
        mx = jnp.max(g, axis=axis, keepdims=True)
        idx = jnp.min(jnp.where(g == mx, blk, n), axis=axis, keepdims=True)
        pick = blk == idx
        sel = sel | (pick & valid)
        g = jnp.where(pick, NEG_INF, g)
    return jnp.where(sel, 0.0, NEG_INF)


def _moba_prompt_kernel(qt_ref, k_ref, vt_ref, km_ref, o_ref, bias_scr, *, nblk, n_tiles):
    i = pl.program_id(2)
    bs = MOBA_BLOCK
    hd = MOBA_HD
    per_tile = LANES // hd
    heads = [(t, h2) for t in range(n_tiles) for h2 in range(per_tile)]
    feat_lane = _iota((1, LANES), 1) // hd
    feat_row = _iota((LANES, 1), 0) // hd
    blk = _iota((nblk, 1), 0)
    causal = _iota((bs, 1), 0) <= _iota((1, bs), 1)
    valid = blk < i

    def key_block(j):
        start = pl.multiple_of(j * bs, bs)
        return k_ref[pl.ds(start, bs), :], vt_ref[0, :, pl.ds(start, bs)].astype(BF16)

    def scores(kj, qbs):
        return [_dot(kj[:, t * LANES:(t + 1) * LANES], qb) for (t, _), qb in zip(heads, qbs)]

    def values(vtj, ps):
        return [_dot(vtj[n * hd:(n + 1) * hd], p.astype(BF16)) for n, p in enumerate(ps)]

    qbs = []
    for n, (t, h2) in enumerate(heads):
        qt = qt_ref[0, t * LANES:(t + 1) * LANES, :]
        km = km_ref[:, t * LANES:(t + 1) * LANES]
        gate = _dot(jnp.where(feat_lane == h2, km, 0.0), qt, precision=_HI)
        bias_scr[n] = _topk_bias(gate, blk, valid, 0)
        qbs.append(jnp.where(feat_row == h2, qt * (hd ** -0.5), 0.0).astype(BF16))

    k_own, vt_own = key_block(i)
    ss = [jnp.where(causal, s, NEG_INF) for s in scores(k_own, qbs)]
    ms = [jnp.max(s, axis=0, keepdims=True) for s in ss]
    ps = [jnp.exp(s - m) for s, m in zip(ss, ms)]
    ls = [jnp.sum(p, axis=0, keepdims=True) for p in ps]
    accs = values(vt_own, ps)

    def body(jj, carry):
        ms, ls, accs = carry
        blocks = [key_block(2 * jj + u) for u in range(2)]
        ss = [[s + bias_scr[n, pl.ds(2 * jj + u, 1), :] for n, s in enumerate(scores(blocks[u][0], qbs))]
              for u in range(2)]
        m_new = [jnp.maximum(m, jnp.maximum(jnp.max(s0, axis=0, keepdims=True), jnp.max(s1, axis=0, keepdims=True)))
                 for m, s0, s1 in zip(ms, ss[0], ss[1])]
        alphas = [jnp.exp(m - mn) for m, mn in zip(ms, m_new)]
        ps = [[jnp.exp(s - mn) for s, mn in zip(ss[u], m_new)] for u in range(2)]
        ls = [a * l + jnp.sum(p0, axis=0, keepdims=True) + jnp.sum(p1, axis=0, keepdims=True)
              for a, l, p0, p1 in zip(alphas, ls, ps[0], ps[1])]
        pvs = [values(blocks[u][1], ps[u]) for u in range(2)]
        accs = [a * acc + pv0 + pv1 for a, acc, pv0, pv1 in zip(alphas, accs, pvs[0], pvs[1])]
        return m_new, ls, accs

    ms, ls, accs = lax.fori_loop(0, (i + 1) // 2, body, (ms, ls, accs))
    ot = jnp.concatenate([acc / l for acc, l in zip(accs, ls)], axis=0)
    o_ref[...] = ot.T.astype(o_ref.dtype)


def _moba_prompt(qt, k, vt, kmean):
    batch, d, s_len = qt.shape
    nblk = s_len // MOBA_BLOCK
    assert s_len % MOBA_BLOCK == 0 and nblk % SUBLANES == 0
    n_tiles = MOBA_TILES_PER_STEP
    width = n_tiles * LANES
    km2 = kmean.reshape(batch * nblk, d)
    return pl.pallas_call(
        functools.partial(_moba_prompt_kernel, nblk=nblk, n_tiles=n_tiles),
        grid=(batch, d // width, nblk),
        in_specs=[pl.BlockSpec((1, width, MOBA_BLOCK), lambda b, h, i: (b, h, i)),
                  pl.BlockSpec((s_len, width), lambda b, h, i: (b, h)),
                  pl.BlockSpec((1, width, s_len), lambda b, h, i: (b, h, 0)),
                  pl.BlockSpec((nblk, width), lambda b, h, i: (b, h))],
        out_specs=pl.BlockSpec((MOBA_BLOCK, width), lambda b, h, i: (b * nblk + i, h)),
        out_shape=jax.ShapeDtypeStruct((batch * s_len, d), BF16),
        scratch_shapes=[pltpu.VMEM((width // MOBA_HD, nblk, MOBA_BLOCK), F32)],
        compiler_params=_params("arbitrary", "arbitrary", "arbitrary"),
        name="moba_prompt",
    )(qt, k, vt, km2)


def _moba_paged_kernel(pt_ref, q_ref, kn_ref, vn_ref, *refs, n_pages, page, t_new):
    k_refs = refs[0:n_pages]
    v_refs = refs[n_pages:2 * n_pages]
    o_ref = refs[2 * n_pages]
    qbd_scr, s_scr, bias_scr, m_scr = refs[2 * n_pages + 1:]
    del pt_ref
    ph = pl.program_id(1)
    ppb = MOBA_BLOCK // page
    n_blk = n_pages // ppb
    n_col = MOBA_HEADS * t_new
    d = D_MODEL
    blk_lane = _iota((1, LANES), 1)
    own_ok = _iota((n_col, t_new), 1) <= _iota((n_col, t_new), 0) % t_new

    def own_scores(qbd):
        return jnp.where(own_ok, _dot_nt(qbd, kn_ref[0].astype(BF16)), NEG_INF)

    def block_bias(bias, p):
        b = p // ppb
        return bias[:, b:b + 1]

    @pl.when(ph == 0)
    def _():
        q = q_ref[0]
        qt = jnp.concatenate([q] * MOBA_HEADS, axis=0)
        bd = _iota((n_col, d), 0) // t_new == _iota((n_col, d), 1) // MOBA_HD
        qbd = jnp.where(bd, qt * (MOBA_HD ** -0.5), 0.0).astype(BF16)
        qbd_scr[...] = qbd
        gate = jnp.zeros((n_col, LANES), F32)
        for b in range(n_blk):
            ssum = None
            for r in range(ppb):
                sc = _dot(qbd, k_refs[b * ppb + r][0].astype(BF16))
                s_scr[b * ppb + r] = sc
                ssum = sc if ssum is None else ssum + sc
            gate = jnp.where(blk_lane == b, jnp.sum(ssum, axis=1, keepdims=True), gate)
        bias = _topk_bias(gate, blk_lane, blk_lane < n_blk, 1)
        bias_scr[...] = bias
        top = None
        for p in range(n_pages):
            sb = s_scr[p] + block_bias(bias, p)
            top = sb if top is None else jnp.maximum(top, sb)
        m = jnp.maximum(jnp.max(own_scores(qbd), axis=1, keepdims=True), jnp.max(top, axis=1, keepdims=True))
        m_scr[...] = jnp.broadcast_to(m, (n_col, LANES))

    @pl.when(ph == 1)
    def _():
        bias = bias_scr[...]
        m = m_scr[:, 0:1]
        e_own = jnp.exp(own_scores(qbd_scr[...]) - m)
        esum = None
        for p in range(n_pages):
            e = jnp.exp(s_scr[p] + block_bias(bias, p) - m)
            s_scr[p] = e
            esum = e if esum is None else esum + e
        inv = 1.0 / (jnp.sum(e_own, axis=1, keepdims=True) + jnp.sum(esum, axis=1, keepdims=True))
        acc = _dot((e_own * inv).astype(BF16), vn_ref[0].astype(BF16))
        for p in range(n_pages):
            acc = acc + _dot_nt((s_scr[p] * inv).astype(BF16), v_refs[p][0].astype(BF16))
        bd = _iota((n_col, d), 0) // t_new == _iota((n_col, d), 1) // MOBA_HD
        a = jnp.where(bd, acc, 0.0)
        out = a[0:t_new]
        for hd in range(1, MOBA_HEADS):
            out = out + a[hd * t_new:(hd + 1) * t_new]
        o_ref[0] = out


def _moba_paged(q, k_new, v_new, cache_kt, cache_vt, page_table):
    db, t_new, d = q.shape
    n_pages = page_table.shape[1]
    page = cache_kt.shape[2]
    past = n_pages * page
    assert past % MOBA_BLOCK == 0 and MOBA_BLOCK % page == 0
    assert t_new <= MOBA_BLOCK and MOBA_TOPK <= past // MOBA_BLOCK <= LANES
    n_col = MOBA_HEADS * t_new
    pt = page_table.reshape(-1).astype(jnp.int32)

    def k_map(r):
        return lambda b, ph, pt_ref: (pt_ref[b * n_pages + r], 0, 0)

    def v_map(r):
        return lambda b, ph, pt_ref: (pt_ref[jnp.where(ph == 0, jnp.maximum(b - 1, 0), b) * n_pages + r], 0, 0)

    tok = pl.BlockSpec((1, t_new, d), lambda b, ph, pt_ref: (b, 0, 0))
    in_specs = [tok, tok, tok]
    in_specs += [pl.BlockSpec((1, d, page), k_map(r)) for r in range(n_pages)]
    in_specs += [pl.BlockSpec((1, d, page), v_map(r)) for r in range(n_pages)]
    grid_spec = pltpu.PrefetchScalarGridSpec(
        num_scalar_prefetch=1,
        grid=(db, 2),
        in_specs=in_specs,
        out_specs=pl.BlockSpec((1, t_new, d), lambda b, ph, pt_ref: (b, 0, 0)),
        scratch_shapes=[pltpu.VMEM((n_col, d), BF16), pltpu.VMEM((n_pages, n_col, page), F32),
                        pltpu.VMEM((n_col, LANES), F32), pltpu.VMEM((n_col, LANES), F32)],
    )
    return pl.pallas_call(
        functools.partial(_moba_paged_kernel, n_pages=n_pages, page=page, t_new=t_new),
        grid_spec=grid_spec,
        out_shape=jax.ShapeDtypeStruct((db, t_new, d), F32),
        compiler_params=_params("arbitrary", "arbitrary"),
        name="moba_paged",
    )(pt, q, k_new, v_new, *([cache_kt] * n_pages), *([cache_vt] * n_pages))


def _feature_major_pages(cache):
    n_phys, page, n_h, hd = cache.shape
    return jnp.transpose(cache, (0, 2, 3, 1)).reshape(n_phys, n_h * hd, page)


def _token_rows(xt, batch):
    _, _, s_len = xt.shape
    return jnp.transpose(xt.reshape(batch, MOBA_HEADS, MOBA_HD, s_len), (0, 3, 1, 2))


def kernel(x_prompt, x_sample, state_ret, cache_k, cache_v, state_conv, page_table, norm_mix, norm_ff, norm_final, ret_w_qkvg, ret_w_o, moba_w_qkv, moba_w_o, conv_w_in, conv_w, conv_w_out, mlp_w_up, mlp_w_down):
    b, s, d = x_prompt.shape
    db, t, _ = x_sample.shape
    depth = norm_mix.shape[0]
    n_pages = page_table.shape[1]
    page = cache_k.shape[2]
    past_len = n_pages * page

    xp = x_prompt.reshape(b * s, d)
    xs = x_sample.reshape(db * t, d)
    tm_p = TM_PROMPT
    tm_s = db * t

    cos_p, sin_p = _rope_tables(jnp.arange(s, dtype=jnp.int32))
    cos_s, sin_s = _rope_tables(past_len + jnp.arange(t, dtype=jnp.int32))
    cos_s = jnp.tile(cos_s, (tm_s // t, 1))
    sin_s = jnp.tile(sin_s, (tm_s // t, 1))
    tabs_p = _ret_tables(RET_CHUNK)
    tabs_s = _ret_tables(t)

    ret_p, ret_s_all, kp_l, vp_l, ks_l, vs_l, conv_p, conv_s = [], None, [], [], [], [], [], []
    for i in range(depth):
        kind = i % N_MIXERS
        j = i // N_MIXERS
        g_mix = norm_mix[i].reshape(1, d)
        g_ff = norm_ff[i].reshape(1, d)
        if kind == 0:
            w = ret_w_qkvg[j].astype(BF16)
            w_o = ret_w_o[j].astype(BF16)
            q, k, v, g = _ret_proj(xp, g_mix, w, cos_p, sin_p, tm=tm_p, qv_dtype=BF16)
            op, st_p = _ret_prompt(q, k, v, g, tabs_p, batch=b, chunk=RET_CHUNK)
            q, k, v, g = _ret_proj(xs, g_mix, w, cos_s, sin_s, tm=tm_s, qv_dtype=F32)
            os_, ret_s_all = _ret_core(q, k, v, g, tabs_s, batch=db, chunk=t, nb=RET_SEQS_PER_STEP,
                                       state_all=state_ret, layer=j, out_all=ret_s_all)
            ret_p.append(st_p)
        elif kind == 1:
            w_o = moba_w_o[j].astype(BF16)
            qt, k, kt, vt, kmean = _moba_proj_t(xp, g_mix, moba_w_qkv[j], batch=b, tm=tm_p)
            op = _moba_prompt(qt, k, vt, kmean)
            kp_l.append(_token_rows(kt, b))
            vp_l.append(_token_rows(vt, b))
            q, k, v = _norm_proj(xs, g_mix, moba_w_qkv[j].astype(BF16), (d, d, d), (F32, F32, F32), tm=tm_s)
            os_ = _moba_paged(q.reshape(db, t, d), k.reshape(db, t, d), v.reshape(db, t, d),
                              _feature_major_pages(cache_k[j]), _feature_major_pages(cache_v[j]),
                              page_table).reshape(db * t, d)
            ks_l.append(k.reshape(db, t, MOBA_HEADS, MOBA_HD))
            vs_l.append(v.reshape(db, t, MOBA_HEADS, MOBA_HD))
        else:
            w_in = conv_w_in[j].astype(BF16)
            w_out = conv_w_out[j].astype(BF16)
            xp, buf_p = _conv_mixer(xp, g_mix, w_in, conv_w[j], w_out, tm=tm_p, seq_len=s)
            xs, buf_s = _conv_mixer(xs, g_mix, w_in, conv_w[j], w_out, tm=tm_s, seq_len=t,
                                    state=state_conv[j])
            conv_p.append(buf_p)
            conv_s.append(buf_s)
            op = os_ = w_o = None
        w_up = mlp_w_up[i].astype(BF16)
        w_down = mlp_w_down[i].astype(BF16)
        g_fin = norm_final.reshape(1, d) if i == depth - 1 else None
        xp = _mlp(xp, g_ff, w_up, w_down, tm=TM_MLP, tf=TF_MLP, a=op, w_o=w_o, final_gamma=g_fin)
        xs = _mlp(xs, g_ff, w_up, w_down, tm=tm_s, tf=TF_MLP, a=os_, w_o=w_o, final_gamma=g_fin)
    y_prompt = xp.reshape(b, s, d)
    y_sample = xs.reshape(db, t, d)
    return (y_prompt, y_sample, jnp.stack(ret_p), ret_s_all, jnp.stack(kp_l), jnp.stack(vp_l),
            jnp.stack(ks_l), jnp.stack(vs_l), jnp.stack(conv_p), jnp.stack(conv_s))
```

```python
import functools

import jax
import jax.numpy as jnp
from jax import lax
from jax.experimental import pallas as pl
from jax.experimental.pallas import tpu as pltpu

F32 = jnp.float32
BF16 = jnp.bfloat16
NEG_INF = float("-inf")

D_MODEL = 1024
N_MIXERS = 3
RET_HEADS = 4
RET_DK = D_MODEL // RET_HEADS
RET_DV = 2 * RET_DK
ROPE_BASE = 10000.0
MOBA_HEADS = 16
MOBA_HD = D_MODEL // MOBA_HEADS
MOBA_BLOCK = 256
MOBA_TOPK = 3
CONV_WIDTH = 3
D_FF = 4 * D_MODEL
EPS = 1e-6
LOG2_E = 1.4426950408889634

LANES = 128
SUBLANES = 8

TM_PROMPT = 512
TM_MLP = 512
TF_MLP = 1024
RET_CHUNK = 256
RET_SEQS_PER_STEP = 8
MOBA_TILES_PER_STEP = 2


def _params(*sem):
    return pltpu.CompilerParams(dimension_semantics=sem)


def _row_tile(m, want):
    t = min(m, want)
    assert m % t == 0, (m, t)
    return t


def _rms(x, gamma):
    ms = jnp.mean(x * x, axis=-1, keepdims=True)
    return x * lax.rsqrt(ms + EPS) * gamma


def _dot(a, b):
    return jnp.dot(a, b, preferred_element_type=F32)


def _dot_nt(a, b):
    return lax.dot_general(a, b, (((1,), (1,)), ((), ())), preferred_element_type=F32)


def _dot_tn(a, b):
    return lax.dot_general(a, b, (((0,), (0,)), ((), ())), preferred_element_type=F32)


def _iota(shape, dim):
    return lax.broadcasted_iota(jnp.int32, shape, dim)


def _norm_proj_kernel(x_ref, g_ref, w_ref, *out_refs, widths):
    h = _rms(x_ref[...], g_ref[...]).astype(BF16)
    off = 0
    for idx, wd in enumerate(widths):
        out_refs[idx][...] = _dot(h, w_ref[:, off:off + wd]).astype(out_refs[idx].dtype)
        off += wd


def _norm_proj(x, gamma, w, widths, dtypes, *, tm):
    m, d = x.shape
    tm = _row_tile(m, tm)
    n = sum(widths)
    return pl.pallas_call(
        functools.partial(_norm_proj_kernel, widths=tuple(widths)),
        grid=(m // tm,),
        in_specs=[pl.BlockSpec((tm, d), lambda i: (i, 0)),
                  pl.BlockSpec((1, d), lambda i: (0, 0)),
                  pl.BlockSpec((d, n), lambda i: (0, 0))],
        out_specs=[pl.BlockSpec((tm, wd), lambda i: (i, 0)) for wd in widths],
        out_shape=[jax.ShapeDtypeStruct((m, wd), dt) for wd, dt in zip(widths, dtypes)],
        compiler_params=_params("arbitrary"),
        name="norm_proj",
    )(x, gamma, w)


def _moba_proj_t_kernel(x_ref, g_ref, wt_ref, wk_ref, qt_ref, k_ref, kt_ref, vt_ref, km_ref):
    d = D_MODEL
    h = _rms(x_ref[...], g_ref[...]).astype(BF16)
    qt_ref[0] = _dot_nt(wt_ref[0:d, :], h)
    kt_ref[0] = _dot_nt(wt_ref[d:2 * d, :], h)
    vt_ref[0] = _dot_nt(wt_ref[2 * d:3 * d, :], h)
    k = _dot(h, wk_ref[...])
    k_ref[...] = k.astype(BF16)
    for r in range(k.shape[0] // MOBA_BLOCK):
        km_ref[r] = jnp.mean(k[r * MOBA_BLOCK:(r + 1) * MOBA_BLOCK], axis=0, keepdims=True)


def _moba_proj_t(x, gamma, w_qkv, *, batch, tm):
    m, d = x.shape
    s_len = m // batch
    tm = _row_tile(s_len, tm)
    assert tm % MOBA_BLOCK == 0
    per_seq = s_len // tm
    w_t = w_qkv.T.astype(BF16)
    w_k = w_qkv[:, d:2 * d].astype(BF16)
    feat = lambda i: (i // per_seq, 0, i % per_seq)
    t_shape = jax.ShapeDtypeStruct((batch, d, s_len), F32)
    return pl.pallas_call(
        _moba_proj_t_kernel,
        grid=(m // tm,),
        in_specs=[pl.BlockSpec((tm, d), lambda i: (i, 0)),
                  pl.BlockSpec((1, d), lambda i: (0, 0)),
                  pl.BlockSpec((3 * d, d), lambda i: (0, 0)),
                  pl.BlockSpec((d, d), lambda i: (0, 0))],
        out_specs=[pl.BlockSpec((1, d, tm), feat),
                   pl.BlockSpec((tm, d), lambda i: (i, 0)),
                   pl.BlockSpec((1, d, tm), feat),
                   pl.BlockSpec((1, d, tm), feat),
                   pl.BlockSpec((tm // MOBA_BLOCK, 1, d), lambda i: (i, 0, 0))],
        out_shape=[t_shape, jax.ShapeDtypeStruct((m, d), BF16), t_shape, t_shape,
                   jax.ShapeDtypeStruct((m // MOBA_BLOCK, 1, d), F32)],
        compiler_params=_params("arbitrary"),
        name="moba_proj_t",
    )(x, gamma, w_t, w_k)


def _ret_proj_kernel(x_ref, g_ref, w_ref, cos_ref, sin_ref, q_ref, k_ref, v_ref, gate_ref):
    h = _rms(x_ref[...], g_ref[...]).astype(BF16)
    hk = RET_HEADS * RET_DK
    hv = RET_HEADS * RET_DV
    even = (_iota((1, LANES), 1) % 2) == 0
    for off, scale, out in ((0, 1.0, q_ref), (hk, RET_DK ** -0.5, k_ref)):
        p = _dot(h, w_ref[:, off:off + hk])
        for c in range(hk // LANES):
            x = p[:, c * LANES:(c + 1) * LANES]
            swapped = jnp.where(even, pltpu.roll(x, LANES - 1, 1), pltpu.roll(x, 1, 1))
            tab = pl.ds((c * LANES) % RET_DK, LANES)
            rot = x * cos_ref[:, tab] + swapped * sin_ref[:, tab]
            out[:, c * LANES:(c + 1) * LANES] = (rot * scale).astype(out.dtype)
    v_ref[...] = _dot(h, w_ref[:, 2 * hk:2 * hk + hv]).astype(v_ref.dtype)
    gate_ref[...] = _dot(h, w_ref[:, 2 * hk + hv:])


def _ret_proj(x, gamma, w_all, layer, cos, sin, *, tm, qv_dtype):
    m, d = x.shape
    tm = _row_tile(m, tm)
    hk = RET_HEADS * RET_DK
    hv = RET_HEADS * RET_DV
    n_tab = cos.shape[0] // tm
    assert cos.shape[0] % tm == 0
    row = lambda i: (i, 0)
    return pl.pallas_call(
        _ret_proj_kernel,
        grid=(m // tm,),
        in_specs=[pl.BlockSpec((tm, d), row),
                  pl.BlockSpec((1, d), lambda i: (0, 0)),
                  pl.BlockSpec((None, d, 2 * hk + 2 * hv), lambda i: (layer, 0, 0)),
                  pl.BlockSpec((tm, RET_DK), lambda i: (i % n_tab, 0)),
                  pl.BlockSpec((tm, RET_DK), lambda i: (i % n_tab, 0))],
        out_specs=[pl.BlockSpec((tm, hk), row), pl.BlockSpec((tm, hk), row),
                   pl.BlockSpec((tm, hv), row), pl.BlockSpec((tm, hv), row)],
        out_shape=[jax.ShapeDtypeStruct((m, hk), qv_dtype), jax.ShapeDtypeStruct((m, hk), F32),
                   jax.ShapeDtypeStruct((m, hv), qv_dtype), jax.ShapeDtypeStruct((m, hv), F32)],
        compiler_params=_params("arbitrary"),
        name="ret_proj",
    )(x, gamma, w_all, cos, sin)


def _rope_tables(pos):
    inv = ROPE_BASE ** (-jnp.arange(0, RET_DK, 2, dtype=F32) / RET_DK)
    ang = pos.astype(F32)[:, None] * inv[None, :]
    cos = jnp.repeat(jnp.cos(ang), 2, axis=1)
    sin = jnp.sin(ang)
    return cos, jnp.stack([-sin, sin], axis=-1).reshape(cos.shape)


def _ret_core_kernel(*refs, nb, chunk, n_chunks, has_state, layer, fill_layers):
    refs = list(refs)
    q_ref, k_ref, v_ref, g_ref, dm_ref, qd_ref, kd_ref = refs[:7]
    st0_ref = refs[7] if has_state else None
    o_ref, stout_ref, st_scr = refs[-3:]
    c = pl.program_id(2)

    @pl.when(c == 0)
    def _():
        for s in range(nb):
            st_scr[s] = st0_ref[s, 0] if has_state else jnp.zeros((RET_DK, RET_DV), F32)

    dm = dm_ref[0]
    qd = qd_ref[0]
    kd = kd_ref[0]
    qd_v = jnp.concatenate([qd] * (RET_DV // LANES), axis=1)
    kd_k = jnp.concatenate([kd] * (RET_DK // LANES), axis=1)
    sdec = qd[chunk - 1:chunk, 0:1]
    for s in range(nb):
        q = q_ref[s].astype(BF16)
        kf = k_ref[s]
        v = v_ref[s].astype(BF16)
        st = st_scr[s]
        sc = _dot_nt(q, kf.astype(BF16)) * dm
        inner = _dot(sc.astype(BF16), v)
        cross = _dot(q, st.astype(BF16)) * qd_v
        o = inner + cross
        kv = _dot_tn((kf * kd_k).astype(BF16), v)
        st_new = sdec * st + kv
        st_scr[s] = st_new
        on = o * lax.rsqrt(jnp.mean(o * o, axis=-1, keepdims=True) + EPS)
        gg = g_ref[s]
        o_ref[s] = (on * (gg * jax.nn.sigmoid(gg))).astype(o_ref.dtype)

        @pl.when(c == n_chunks - 1)
        def _():
            if fill_layers:
                for jj in range(fill_layers):
                    stout_ref[jj, s, 0] = st_new if jj == layer else jnp.zeros_like(st_new)
            else:
                stout_ref[s, 0] = st_new


def _ret_core(q, k, v, g, tabs, *, batch, chunk, nb, state_all=None, layer=0, out_all=None):
    hk = RET_HEADS * RET_DK
    hv = RET_HEADS * RET_DV
    s_len = q.shape[0] // batch
    n_chunks = s_len // chunk
    assert s_len % chunk == 0 and batch % nb == 0
    dm, qd, kd = tabs
    seq = lambda b, h, c: (b, c, h)
    tab = lambda b, h, c: (h, 0, 0)
    in_specs = [pl.BlockSpec((nb, chunk, RET_DK), seq), pl.BlockSpec((nb, chunk, RET_DK), seq),
                pl.BlockSpec((nb, chunk, RET_DV), seq), pl.BlockSpec((nb, chunk, RET_DV), seq),
                pl.BlockSpec((1, chunk, chunk), tab), pl.BlockSpec((1, chunk, LANES), tab),
                pl.BlockSpec((1, chunk, LANES), tab)]
    args = [q.reshape(batch, s_len, hk), k.reshape(batch, s_len, hk),
            v.reshape(batch, s_len, hv), g.reshape(batch, s_len, hv), dm, qd, kd]
    aliases = {}
    fill_layers = 0
    if state_all is None:
        st_shape = (batch, RET_HEADS, RET_DK, RET_DV)
        st_spec = pl.BlockSpec((nb, 1, RET_DK, RET_DV), lambda b, h, c: (b, h, 0, 0))
    else:
        n_layers = state_all.shape[0]
        st_shape = state_all.shape
        in_specs.append(pl.BlockSpec((None, nb, 1, RET_DK, RET_DV), lambda b, h, c: (layer, b, h, 0, 0)))
        args.append(state_all)
        if out_all is None:
            fill_layers = n_layers
            st_spec = pl.BlockSpec((n_layers, nb, 1, RET_DK, RET_DV), lambda b, h, c: (0, b, h, 0, 0))
        else:
            in_specs.append(pl.BlockSpec(memory_space=pl.ANY))
            args.append(out_all)
            aliases = {len(args) - 1: 1}
            st_spec = pl.BlockSpec((None, nb, 1, RET_DK, RET_DV), lambda b, h, c: (layer, b, h, 0, 0))
    kern = functools.partial(_ret_core_kernel, nb=nb, chunk=chunk, n_chunks=n_chunks,
                             has_state=state_all is not None, layer=layer, fill_layers=fill_layers)
    if aliases:
        inner = kern
        n_in = len(args)
        kern = lambda *refs: inner(*refs[:n_in - 1], *refs[n_in:])
    o, st = pl.pallas_call(
        kern,
        grid=(batch // nb, RET_HEADS, n_chunks),
        in_specs=in_specs,
        out_specs=[pl.BlockSpec((nb, chunk, RET_DV), seq), st_spec],
        out_shape=[jax.ShapeDtypeStruct((batch, s_len, hv), BF16),
                   jax.ShapeDtypeStruct(st_shape, F32)],
        scratch_shapes=[pltpu.VMEM((nb, RET_DK, RET_DV), F32)],
        input_output_aliases=aliases,
        compiler_params=_params("arbitrary", "arbitrary", "arbitrary"),
        name="ret_core",
    )(*args)
    return o.reshape(batch * s_len, hv), st


def _ret_prompt_kernel(q_ref, k_ref, v_ref, g_ref, dm_ref, qd_ref, kd_ref, o_ref, stout_ref, st_scr, *,
                       chunk, n_chunks):
    c = pl.program_id(1)
    dk, dv = RET_DK, RET_DV

    @pl.when(c == 0)
    def _():
        st_scr[...] = jnp.zeros((RET_HEADS, dk, dv), F32)

    for h in range(RET_HEADS):
        q = q_ref[0, :, h * dk:(h + 1) * dk]
        kf = k_ref[0, :, h * dk:(h + 1) * dk]
        v = v_ref[0, :, h * dv:(h + 1) * dv]
        st = st_scr[h]
        qd = qd_ref[h]
        kd_k = jnp.concatenate([kd_ref[h]] * (dk // LANES), axis=1)
        sc = _dot_nt(q, kf.astype(BF16))
        cross = _dot(q, st.astype(BF16))
        kv = _dot_tn((kf * kd_k).astype(BF16), v)
        inner = _dot((sc * dm_ref[h]).astype(BF16), v)
        o = inner + cross * jnp.concatenate([qd] * (dv // LANES), axis=1)
        st_scr[h] = qd[chunk - 1:chunk, 0:1] * st + kv
        on = o * lax.rsqrt(jnp.mean(o * o, axis=-1, keepdims=True) + EPS)
        gg = g_ref[0, :, h * dv:(h + 1) * dv]
        o_ref[0, :, h * dv:(h + 1) * dv] = (on * (gg * jax.nn.sigmoid(gg))).astype(o_ref.dtype)

    @pl.when(c == n_chunks - 1)
    def _():
        stout_ref[0] = st_scr[...]


def _ret_prompt(q, k, v, g, tabs, *, batch, chunk):
    hk = RET_HEADS * RET_DK
    hv = RET_HEADS * RET_DV
    s_len = q.shape[0] // batch
    n_chunks = s_len // chunk
    assert s_len % chunk == 0
    dm, qd, kd = tabs
    seq = lambda b, c: (b, c, 0)
    tab = lambda b, c: (0, 0, 0)
    o, st = pl.pallas_call(
        functools.partial(_ret_prompt_kernel, chunk=chunk, n_chunks=n_chunks),
        grid=(batch, n_chunks),
        in_specs=[pl.BlockSpec((1, chunk, hk), seq), pl.BlockSpec((1, chunk, hk), seq),
                  pl.BlockSpec((1, chunk, hv), seq), pl.BlockSpec((1, chunk, hv), seq),
                  pl.BlockSpec((RET_HEADS, chunk, chunk), tab), pl.BlockSpec((RET_HEADS, chunk, LANES), tab),
                  pl.BlockSpec((RET_HEADS, chunk, LANES), tab)],
        out_specs=[pl.BlockSpec((1, chunk, hv), seq),
                   pl.BlockSpec((1, RET_HEADS, RET_DK, RET_DV), lambda b, c: (b, 0, 0, 0))],
        out_shape=[jax.ShapeDtypeStruct((batch, s_len, hv), BF16),
                   jax.ShapeDtypeStruct((batch, RET_HEADS, RET_DK, RET_DV), F32)],
        scratch_shapes=[pltpu.VMEM((RET_HEADS, RET_DK, RET_DV), F32)],
        compiler_params=_params("arbitrary", "arbitrary"),
        name="ret_prompt",
    )(q.reshape(batch, s_len, hk), k.reshape(batch, s_len, hk), v.reshape(batch, s_len, hv),
      g.reshape(batch, s_len, hv), dm, qd, kd)
    return o.reshape(batch * s_len, hv), st


def _ret_tables(chunk):
    log_g = jnp.log(1.0 - 2.0 ** (-5.0 - jnp.arange(RET_HEADS, dtype=F32)))
    i = jnp.arange(chunk, dtype=F32)
    diff = i[:, None] - i[None, :]
    dm = jnp.where(diff >= 0, jnp.exp(log_g[:, None, None] * jnp.maximum(diff, 0.0)), 0.0)
    qd = jnp.exp(log_g[:, None] * (i[None, :] + 1.0))
    kd = jnp.exp(log_g[:, None] * (chunk - 1.0 - i[None, :]))
    rep = lambda t: jnp.broadcast_to(t[:, :, None], (RET_HEADS, chunk, LANES))
    return dm, rep(qd), rep(kd)


def _mlp_kernel(*refs, tf, has_proj, has_final):
    refs = list(refs)
    x_ref = refs.pop(0)
    a_ref, wo_ref = (refs.pop(0), refs.pop(0)) if has_proj else (None, None)
    g_ref, wu_ref, wd_ref = refs.pop(0), refs.pop(0), refs.pop(0)
    gf_ref = refs.pop(0) if has_final else None
    o_ref, = refs
    x = x_ref[...]
    if has_proj:
        x = x + _dot(a_ref[...].astype(BF16), wo_ref[...])
    h = _rms(x, g_ref[...]).astype(BF16)
    acc = None
    for f in range(D_FF // tf):
        a = jnp.maximum(_dot(h, wu_ref[:, f * tf:(f + 1) * tf]), 0.0)
        part = _dot((a * a).astype(BF16), wd_ref[f * tf:(f + 1) * tf, :])
        acc = part if acc is None else acc + part
    y = x + acc
    o_ref[...] = _rms(y, gf_ref[...]) if has_final else y


def _layer_weight(w_all, layer):
    return pl.BlockSpec((None,) + w_all.shape[1:], lambda i: (layer, 0, 0), pipeline_mode=pl.Buffered(1))


def _mlp(x, gamma, w_up_all, w_down_all, layer, *, tm, tf, a=None, w_o_all=None, w_o_layer=0, final_gamma=None):
    m, d = x.shape
    tm = _row_tile(m, tm)
    row = lambda i: (i, 0)
    const = lambda i: (0, 0)
    in_specs = [pl.BlockSpec((tm, d), row)]
    args = [x]
    if a is not None:
        in_specs += [pl.BlockSpec((tm, a.shape[1]), row), _layer_weight(w_o_all, w_o_layer)]
        args += [a, w_o_all]
    in_specs += [pl.BlockSpec((1, d), const), _layer_weight(w_up_all, layer), _layer_weight(w_down_all, layer)]
    args += [gamma, w_up_all, w_down_all]
    if final_gamma is not None:
        in_specs.append(pl.BlockSpec((1, d), const))
        args.append(final_gamma)
    return pl.pallas_call(
        functools.partial(_mlp_kernel, tf=tf, has_proj=a is not None, has_final=final_gamma is not None),
        grid=(m // tm,),
        in_specs=in_specs,
        out_specs=pl.BlockSpec((tm, d), row),
        out_shape=jax.ShapeDtypeStruct((m, d), F32),
        compiler_params=_params("arbitrary"),
        name="mlp",
    )(*args)


def _conv_kernel(*refs, tiles_per_seq, seq_in_tile):
    if seq_in_tile:
        x_ref, g_ref, win_ref, wc_ref, wout_ref, p1_ref, p2_ref, o_ref, u_ref = refs
    else:
        x_ref, g_ref, win_ref, wc_ref, wout_ref, o_ref, u_ref, carry_scr = refs
    d = D_MODEL
    x = x_ref[...]
    tm = x.shape[0]
    h = _rms(x, g_ref[...]).astype(BF16)
    bg = _dot(h, win_ref[:, 0:d])
    u = _dot(h, win_ref[:, d:2 * d]) * _dot(h, win_ref[:, 2 * d:3 * d])
    row = _iota((tm, 1), 0)
    r1 = pltpu.roll(u, 1, 0)
    r2 = pltpu.roll(u, 2, 0)
    if seq_in_tile:
        t = row % seq_in_tile
        um1 = jnp.where(t >= 1, r1, 0.0) + p1_ref[...]
        um2 = jnp.where(t >= 2, r2, 0.0) + p2_ref[...]
        u_ref[...] = u
    else:
        @pl.when((pl.program_id(0) % tiles_per_seq) == 0)
        def _():
            carry_scr[...] = jnp.zeros((SUBLANES, d), F32)

        carry = carry_scr[...]
        c1 = carry[SUBLANES - 1:SUBLANES]
        c2 = carry[SUBLANES - 2:SUBLANES - 1]
        um1 = jnp.where(row >= 1, r1, c1)
        um2 = jnp.where(row >= 2, r2, jnp.where(row == 1, c1, c2))
        tail = u[tm - SUBLANES:tm]
        carry_scr[...] = tail
        u_ref[...] = tail
    wc = wc_ref[...]
    y = wc[0:1] * um2 + wc[1:2] * um1 + wc[2:3] * u
    o_ref[...] = x + _dot((bg * y).astype(BF16), wout_ref[...])


def _conv_mixer(x, gamma, w_in, w_c, w_out, *, tm, seq_len, state=None):
    m, d = x.shape
    n_seq = m // seq_len
    wc_pad = jnp.zeros((SUBLANES, d), F32).at[:CONV_WIDTH].set(w_c)
    row = lambda i: (i, 0)
    const = lambda i: (0, 0)
    in_specs = [pl.BlockSpec((tm, d), row), pl.BlockSpec((1, d), const),
                pl.BlockSpec((d, 3 * d), const), pl.BlockSpec((SUBLANES, d), const),
                pl.BlockSpec((d, d), const)]
    args = [x, gamma, w_in, wc_pad, w_out]
    if state is not None:
        tm = _row_tile(m, tm)
        assert tm % seq_len == 0 and seq_len >= CONV_WIDTH - 1
        zeros = jnp.zeros((n_seq, seq_len, d), F32)
        p1 = zeros.at[:, 0].set(state[:, 1]).reshape(m, d)
        p2 = zeros.at[:, 0].set(state[:, 0]).at[:, 1].set(state[:, 1]).reshape(m, d)
        in_specs += [pl.BlockSpec((tm, d), row), pl.BlockSpec((tm, d), row)]
        args += [p1, p2]
        u_rows, u_blk, scratch = m, tm, []
        kern = functools.partial(_conv_kernel, tiles_per_seq=0, seq_in_tile=seq_len)
    else:
        tm = _row_tile(seq_len, tm)
        u_rows, u_blk = (m // tm) * SUBLANES, SUBLANES
        scratch = [pltpu.VMEM((SUBLANES, d), F32)]
        kern = functools.partial(_conv_kernel, tiles_per_seq=seq_len // tm, seq_in_tile=0)
    out, u = pl.pallas_call(
        kern,
        grid=(m // tm,),
        in_specs=in_specs,
        out_specs=[pl.BlockSpec((tm, d), row), pl.BlockSpec((u_blk, d), row)],
        out_shape=[jax.ShapeDtypeStruct((m, d), F32), jax.ShapeDtypeStruct((u_rows, d), F32)],
        scratch_shapes=scratch,
        compiler_params=_params("arbitrary"),
        name="conv_mixer",
    )(*args)
    u_last = u.reshape(n_seq, -1, d)[:, -(CONV_WIDTH - 1):]
    return out, u_last


def _topk_bias(gate, blk, valid, axis):
    n = gate.shape[axis]
    g = jnp.where(valid, gate, NEG_INF)
    sel = jnp.zeros(gate.shape, jnp.bool_)
    for _ in range(MOBA_TOPK):
        mx = jnp.max(g, axis=axis, keepdims=True)
        idx = jnp.min(jnp.where(g == mx, blk, n), axis=axis, keepdims=True)
        pick = blk == idx
        sel = sel | (pick & valid)
        g = jnp.where(pick, NEG_INF, g)
    return jnp.where(sel, 0.0, NEG_INF)


def _moba_prompt_kernel(qt_ref, k_ref, vt_ref, km_ref, o_ref, bias_scr, *, nblk, n_tiles):
    i = pl.program_id(2)
    bs = MOBA_BLOCK
    hd = MOBA_HD
    per_tile = LANES // hd
    heads = [(t, h2) for t in range(n_tiles) for h2 in range(per_tile)]
    feat_row = _iota((LANES, 1), 0) // hd
    blk = _iota((nblk, 1), 0)
    causal = _iota((bs, 1), 0) <= _iota((1, bs), 1)
    valid = blk < i

    def key_block(j):
        start = pl.multiple_of(j * bs, bs)
        return k_ref[pl.ds(start, bs), :], vt_ref[0, :, pl.ds(start, bs)].astype(BF16)

    def scores(kj, qbs):
        return [_dot(kj[:, t * LANES:(t + 1) * LANES], qb) for (t, _), qb in zip(heads, qbs)]

    def values(vtj, ps):
        return [_dot(vtj[n * hd:(n + 1) * hd], p.astype(BF16)) for n, p in enumerate(ps)]

    qbs = []
    for n, (t, h2) in enumerate(heads):
        qt = qt_ref[0, t * LANES:(t + 1) * LANES, :]
        km = km_ref[:, t * LANES:(t + 1) * LANES]
        qb = jnp.where(feat_row == h2, qt * (hd ** -0.5 * LOG2_E), 0.0).astype(BF16)
        qbs.append(qb)
        bias_scr[n] = _topk_bias(_dot(km.astype(BF16), qb), blk, valid, 0)

    k_own, vt_own = key_block(i)
    ss = [jnp.where(causal, s, NEG_INF) for s in scores(k_own, qbs)]
    ms = [jnp.max(s, axis=0, keepdims=True) for s in ss]
    ps = [jnp.exp2(s - m) for s, m in zip(ss, ms)]
    ls = [jnp.sum(p, axis=0, keepdims=True) for p in ps]
    accs = values(vt_own, ps)

    def body(jj, carry):
        ms, ls, accs = carry
        blocks = [key_block(2 * jj + u) for u in range(2)]
        ss = [scores(blocks[u][0], qbs) for u in range(2)]
        biases = [[bias_scr[n, pl.ds(2 * jj + u, 1), :] for n in range(len(heads))] for u in range(2)]
        tops = [[jnp.max(s, axis=0, keepdims=True) + bias for s, bias in zip(ss[u], biases[u])] for u in range(2)]
        m_new = [jnp.maximum(m, jnp.maximum(t0, t1)) for m, t0, t1 in zip(ms, tops[0], tops[1])]
        alphas = [jnp.exp2(m - mn) for m, mn in zip(ms, m_new)]
        ps = [[jnp.exp2(s - (mn - bias)) for s, mn, bias in zip(ss[u], m_new, biases[u])] for u in range(2)]
        ls = [a * l + jnp.sum(p0, axis=0, keepdims=True) + jnp.sum(p1, axis=0, keepdims=True)
              for a, l, p0, p1 in zip(alphas, ls, ps[0], ps[1])]
        pvs = [values(blocks[u][1], ps[u]) for u in range(2)]
        accs = [a * acc + pv0 + pv1 for a, acc, pv0, pv1 in zip(alphas, accs, pvs[0], pvs[1])]
        return m_new, ls, accs

    ms, ls, accs = lax.fori_loop(0, (i + 1) // 2, body, (ms, ls, accs))
    ot = jnp.concatenate([acc / l for acc, l in zip(accs, ls)], axis=0)
    o_ref[...] = ot.T.astype(o_ref.dtype)


def _moba_prompt(qt, k, vt, kmean):
    batch, d, s_len = qt.shape
    nblk = s_len // MOBA_BLOCK
    assert s_len % MOBA_BLOCK == 0 and nblk % SUBLANES == 0
    n_tiles = MOBA_TILES_PER_STEP
    width = n_tiles * LANES
    km2 = kmean.reshape(batch * nblk, d)
    return pl.pallas_call(
        functools.partial(_moba_prompt_kernel, nblk=nblk, n_tiles=n_tiles),
        grid=(batch, d // width, nblk),
        in_specs=[pl.BlockSpec((1, width, MOBA_BLOCK), lambda b, h, i: (b, h, i)),
                  pl.BlockSpec((s_len, width), lambda b, h, i: (b, h)),
                  pl.BlockSpec((1, width, s_len), lambda b, h, i: (b, h, 0)),
                  pl.BlockSpec((nblk, width), lambda b, h, i: (b, h))],
        out_specs=pl.BlockSpec((MOBA_BLOCK, width), lambda b, h, i: (b * nblk + i, h)),
        out_shape=jax.ShapeDtypeStruct((batch * s_len, d), BF16),
        scratch_shapes=[pltpu.VMEM((width // MOBA_HD, nblk, MOBA_BLOCK), F32)],
        compiler_params=_params("arbitrary", "arbitrary", "arbitrary"),
        name="moba_prompt",
    )(qt, k, vt, km2)


def _moba_paged_kernel(pt_ref, q_ref, kn_ref, vn_ref, *refs, n_pages, page, t_new):
    k_refs = refs[0:n_pages]
    v_refs = refs[n_pages:2 * n_pages]
    o_ref = refs[2 * n_pages]
    qbd_scr, s_scr, bias_scr, m_scr = refs[2 * n_pages + 1:]
    del pt_ref
    ph = pl.program_id(1)
    ppb = MOBA_BLOCK // page
    n_blk = n_pages // ppb
    n_col = MOBA_HEADS * t_new
    d = D_MODEL
    blk_lane = _iota((1, LANES), 1)
    own_ok = _iota((n_col, t_new), 1) <= _iota((n_col, t_new), 0) % t_new

    def own_scores(qbd):
        return jnp.where(own_ok, _dot_nt(qbd, kn_ref[0].astype(BF16)), NEG_INF)

    def block_bias(bias, p):
        b = p // ppb
        return bias[:, b:b + 1]

    @pl.when(ph == 0)
    def _():
        q = q_ref[0]
        qt = jnp.concatenate([q] * MOBA_HEADS, axis=0)
        bd = _iota((n_col, d), 0) // t_new == _iota((n_col, d), 1) // MOBA_HD
        qbd = jnp.where(bd, qt * (MOBA_HD ** -0.5), 0.0).astype(BF16)
        qbd_scr[...] = qbd
        gate = jnp.zeros((n_col, LANES), F32)
        for b in range(n_blk):
            ssum = None
            for r in range(ppb):
                sc = _dot(qbd, k_refs[b * ppb + r][0].astype(BF16))
                s_scr[b * ppb + r] = sc
                ssum = sc if ssum is None else ssum + sc
            gate = jnp.where(blk_lane == b, jnp.sum(ssum, axis=1, keepdims=True), gate)
        bias = _topk_bias(gate, blk_lane, blk_lane < n_blk, 1)
        bias_scr[...] = bias
        top = None
        for p in range(n_pages):
            sb = s_scr[p] + block_bias(bias, p)
            top = sb if top is None else jnp.maximum(top, sb)
        m = jnp.maximum(jnp.max(own_scores(qbd), axis=1, keepdims=True), jnp.max(top, axis=1, keepdims=True))
        m_scr[...] = jnp.broadcast_to(m, (n_col, LANES))

    @pl.when(ph == 1)
    def _():
        bias = bias_scr[...]
        m = m_scr[:, 0:1]
        e_own = jnp.exp(own_scores(qbd_scr[...]) - m)
        esum = None
        for p in range(n_pages):
            e = jnp.exp(s_scr[p] + block_bias(bias, p) - m)
            s_scr[p] = e
            esum = e if esum is None else esum + e
        inv = 1.0 / (jnp.sum(e_own, axis=1, keepdims=True) + jnp.sum(esum, axis=1, keepdims=True))
        acc = _dot((e_own * inv).astype(BF16), vn_ref[0].astype(BF16))
        for p in range(n_pages):
            acc = acc + _dot_nt((s_scr[p] * inv).astype(BF16), v_refs[p][0].astype(BF16))
        bd = _iota((n_col, d), 0) // t_new == _iota((n_col, d), 1) // MOBA_HD
        a = jnp.where(bd, acc, 0.0)
        out = a[0:t_new]
        for hd in range(1, MOBA_HEADS):
            out = out + a[hd * t_new:(hd + 1) * t_new]
        o_ref[0] = out


def _moba_paged(q, k_new, v_new, cache_kt, cache_vt, page_table):
    db, t_new, d = q.shape
    n_pages = page_table.shape[1]
    page = cache_kt.shape[2]
    past = n_pages * page
    assert past % MOBA_BLOCK == 0 and MOBA_BLOCK % page == 0
    assert t_new <= MOBA_BLOCK and MOBA_TOPK <= past // MOBA_BLOCK <= LANES
    n_col = MOBA_HEADS * t_new
    pt = page_table.reshape(-1).astype(jnp.int32)

    def k_map(r):
        return lambda b, ph, pt_ref: (pt_ref[b * n_pages + r], 0, 0)

    def v_map(r):
        return lambda b, ph, pt_ref: (pt_ref[jnp.where(ph == 0, jnp.maximum(b - 1, 0), b) * n_pages + r], 0, 0)

    tok = pl.BlockSpec((1, t_new, d), lambda b, ph, pt_ref: (b, 0, 0))
    in_specs = [tok, tok, tok]
    in_specs += [pl.BlockSpec((1, d, page), k_map(r)) for r in range(n_pages)]
    in_specs += [pl.BlockSpec((1, d, page), v_map(r)) for r in range(n_pages)]
    grid_spec = pltpu.PrefetchScalarGridSpec(
        num_scalar_prefetch=1,
        grid=(db, 2),
        in_specs=in_specs,
        out_specs=pl.BlockSpec((1, t_new, d), lambda b, ph, pt_ref: (b, 0, 0)),
        scratch_shapes=[pltpu.VMEM((n_col, d), BF16), pltpu.VMEM((n_pages, n_col, page), F32),
                        pltpu.VMEM((n_col, LANES), F32), pltpu.VMEM((n_col, LANES), F32)],
    )
    return pl.pallas_call(
        functools.partial(_moba_paged_kernel, n_pages=n_pages, page=page, t_new=t_new),
        grid_spec=grid_spec,
        out_shape=jax.ShapeDtypeStruct((db, t_new, d), F32),
        compiler_params=_params("arbitrary", "arbitrary"),
        name="moba_paged",
    )(pt, q, k_new, v_new, *([cache_kt] * n_pages), *([cache_vt] * n_pages))


def _feature_major_pages(cache):
    n_phys, page, n_h, hd = cache.shape
    return jnp.transpose(cache, (0, 2, 3, 1)).reshape(n_phys, n_h * hd, page)


def _token_rows(xt, batch):
    _, _, s_len = xt.shape
    return jnp.transpose(xt.reshape(batch, MOBA_HEADS, MOBA_HD, s_len), (0, 3, 1, 2))


def kernel(x_prompt, x_sample, state_ret, cache_k, cache_v, state_conv, page_table, norm_mix, norm_ff, norm_final, ret_w_qkvg, ret_w_o, moba_w_qkv, moba_w_o, conv_w_in, conv_w, conv_w_out, mlp_w_up, mlp_w_down):
    b, s, d = x_prompt.shape
    db, t, _ = x_sample.shape
    depth = norm_mix.shape[0]
    n_pages = page_table.shape[1]
    page = cache_k.shape[2]
    past_len = n_pages * page

    xp = x_prompt.reshape(b * s, d)
    xs = x_sample.reshape(db * t, d)
    tm_p = TM_PROMPT
    tm_s = db * t

    cos_p, sin_p = _rope_tables(jnp.arange(s, dtype=jnp.int32))
    cos_s, sin_s = _rope_tables(past_len + jnp.arange(t, dtype=jnp.int32))
    cos_s = jnp.tile(cos_s, (tm_s // t, 1))
    sin_s = jnp.tile(sin_s, (tm_s // t, 1))
    tabs_p = _ret_tables(RET_CHUNK)
    tabs_s = _ret_tables(t)

    ret_w = ret_w_qkvg.astype(BF16)
    ret_wo = ret_w_o.astype(BF16)
    moba_wo = moba_w_o.astype(BF16)
    up_w = mlp_w_up.astype(BF16)
    down_w = mlp_w_down.astype(BF16)

    ret_p, ret_s_all, kp_l, vp_l, ks_l, vs_l, conv_p, conv_s = [], None, [], [], [], [], [], []
    for i in range(depth):
        kind = i % N_MIXERS
        j = i // N_MIXERS
        g_mix = norm_mix[i].reshape(1, d)
        g_ff = norm_ff[i].reshape(1, d)
        if kind == 0:
            wo_all = ret_wo
            q, k, v, g = _ret_proj(xp, g_mix, ret_w, j, cos_p, sin_p, tm=tm_p, qv_dtype=BF16)
            op, st_p = _ret_prompt(q, k, v, g, tabs_p, batch=b, chunk=RET_CHUNK)
            q, k, v, g = _ret_proj(xs, g_mix, ret_w, j, cos_s, sin_s, tm=tm_s, qv_dtype=F32)
            os_, ret_s_all = _ret_core(q, k, v, g, tabs_s, batch=db, chunk=t, nb=RET_SEQS_PER_STEP,
                                       state_all=state_ret, layer=j, out_all=ret_s_all)
            ret_p.append(st_p)
        elif kind == 1:
            wo_all = moba_wo
            qt, k, kt, vt, kmean = _moba_proj_t(xp, g_mix, moba_w_qkv[j], batch=b, tm=tm_p)
            op = _moba_prompt(qt, k, vt, kmean)
            kp_l.append(_token_rows(kt, b))
            vp_l.append(_token_rows(vt, b))
            q, k, v = _norm_proj(xs, g_mix, moba_w_qkv[j].astype(BF16), (d, d, d), (F32, F32, F32), tm=tm_s)
            os_ = _moba_paged(q.reshape(db, t, d), k.reshape(db, t, d), v.reshape(db, t, d),
                              _feature_major_pages(cache_k[j]), _feature_major_pages(cache_v[j]),
                              page_table).reshape(db * t, d)
            ks_l.append(k.reshape(db, t, MOBA_HEADS, MOBA_HD))
            vs_l.append(v.reshape(db, t, MOBA_HEADS, MOBA_HD))
        else:
            w_in = conv_w_in[j].astype(BF16)
            w_out = conv_w_out[j].astype(BF16)
            xp, buf_p = _conv_mixer(xp, g_mix, w_in, conv_w[j], w_out, tm=tm_p, seq_len=s)
            xs, buf_s = _conv_mixer(xs, g_mix, w_in, conv_w[j], w_out, tm=tm_s, seq_len=t,
                                    state=state_conv[j])
            conv_p.append(buf_p)
            conv_s.append(buf_s)
            op = os_ = wo_all = None
        g_fin = norm_final.reshape(1, d) if i == depth - 1 else None
        xp = _mlp(xp, g_ff, up_w, down_w, i, tm=TM_MLP, tf=TF_MLP, a=op, w_o_all=wo_all, w_o_layer=j,
                  final_gamma=g_fin)
        xs = _mlp(xs, g_ff, up_w, down_w, i, tm=tm_s, tf=TF_MLP, a=os_, w_o_all=wo_all, w_o_layer=j,
                  final_gamma=g_fin)
    y_prompt = xp.reshape(b, s, d)
    y_sample = xs.reshape(db, t, d)
    return (y_prompt, y_sample, jnp.stack(ret_p), ret_s_all, jnp.stack(kp_l), jnp.stack(vp_l),
            jnp.stack(ks_l), jnp.stack(vs_l), jnp.stack(conv_p), jnp.stack(conv_s))
```

```python
import functools

import jax
import jax.numpy as jnp
from jax import lax
from jax.experimental import pallas as pl
from jax.experimental.pallas import tpu as pltpu

F32 = jnp.float32
BF16 = jnp.bfloat16
NEG_INF = float("-inf")

D_MODEL = 1024
N_MIXERS = 3
RET_HEADS = 4
RET_DK = D_MODEL // RET_HEADS
RET_DV = 2 * RET_DK
ROPE_BASE = 10000.0
MOBA_HEADS = 16
MOBA_HD = D_MODEL // MOBA_HEADS
MOBA_BLOCK = 256
MOBA_TOPK = 3
CONV_WIDTH = 3
D_FF = 4 * D_MODEL
EPS = 1e-6
LOG2_E = 1.4426950408889634

LANES = 128
SUBLANES = 8

TM_PROMPT = 512
TM_MLP = 512
TF_MLP = 1024
RET_CHUNK = 256
RET_SEQS_PER_STEP = 8
MOBA_TILES_PER_STEP = 4


def _params(*sem):
    return pltpu.CompilerParams(dimension_semantics=sem)


def _row_tile(m, want):
    t = min(m, want)
    assert m % t == 0, (m, t)
    return t


def _rms(x, gamma):
    ms = jnp.mean(x * x, axis=-1, keepdims=True)
    return x * lax.rsqrt(ms + EPS) * gamma


def _dot(a, b):
    return jnp.dot(a, b, preferred_element_type=F32)


def _dot_nt(a, b):
    return lax.dot_general(a, b, (((1,), (1,)), ((), ())), preferred_element_type=F32)


def _dot_tn(a, b):
    return lax.dot_general(a, b, (((0,), (0,)), ((), ())), preferred_element_type=F32)


def _iota(shape, dim):
    return lax.broadcasted_iota(jnp.int32, shape, dim)


def _norm_proj_kernel(x_ref, g_ref, w_ref, *out_refs, widths):
    h = _rms(x_ref[...], g_ref[...]).astype(BF16)
    off = 0
    for idx, wd in enumerate(widths):
        out_refs[idx][...] = _dot(h, w_ref[:, off:off + wd]).astype(out_refs[idx].dtype)
        off += wd


def _norm_proj(x, gamma, w, widths, dtypes, *, tm):
    m, d = x.shape
    tm = _row_tile(m, tm)
    n = sum(widths)
    return pl.pallas_call(
        functools.partial(_norm_proj_kernel, widths=tuple(widths)),
        grid=(m // tm,),
        in_specs=[pl.BlockSpec((tm, d), lambda i: (i, 0)),
                  pl.BlockSpec((1, d), lambda i: (0, 0)),
                  pl.BlockSpec((d, n), lambda i: (0, 0))],
        out_specs=[pl.BlockSpec((tm, wd), lambda i: (i, 0)) for wd in widths],
        out_shape=[jax.ShapeDtypeStruct((m, wd), dt) for wd, dt in zip(widths, dtypes)],
        compiler_params=_params("arbitrary"),
        name="norm_proj",
    )(x, gamma, w)


def _moba_proj_t_kernel(x_ref, g_ref, wt_ref, wk_ref, qt_ref, k_ref, kt_ref, vt_ref, km_ref):
    d = D_MODEL
    h = _rms(x_ref[...], g_ref[...]).astype(BF16)
    qt_ref[0] = _dot_nt(wt_ref[0:d, :], h)
    kt_ref[0] = _dot_nt(wt_ref[d:2 * d, :], h)
    vt_ref[0] = _dot_nt(wt_ref[2 * d:3 * d, :], h)
    k = _dot(h, wk_ref[...])
    k_ref[...] = k.astype(BF16)
    for r in range(k.shape[0] // MOBA_BLOCK):
        km_ref[r] = jnp.mean(k[r * MOBA_BLOCK:(r + 1) * MOBA_BLOCK], axis=0, keepdims=True)


def _moba_proj_t(x, gamma, w_qkv, *, batch, tm):
    m, d = x.shape
    s_len = m // batch
    tm = _row_tile(s_len, tm)
    assert tm % MOBA_BLOCK == 0
    per_seq = s_len // tm
    w_t = w_qkv.T.astype(BF16)
    w_k = w_qkv[:, d:2 * d].astype(BF16)
    feat = lambda i: (i // per_seq, 0, i % per_seq)
    t_shape = jax.ShapeDtypeStruct((batch, d, s_len), F32)
    return pl.pallas_call(
        _moba_proj_t_kernel,
        grid=(m // tm,),
        in_specs=[pl.BlockSpec((tm, d), lambda i: (i, 0)),
                  pl.BlockSpec((1, d), lambda i: (0, 0)),
                  pl.BlockSpec((3 * d, d), lambda i: (0, 0)),
                  pl.BlockSpec((d, d), lambda i: (0, 0))],
        out_specs=[pl.BlockSpec((1, d, tm), feat),
                   pl.BlockSpec((tm, d), lambda i: (i, 0)),
                   pl.BlockSpec((1, d, tm), feat),
                   pl.BlockSpec((1, d, tm), feat),
                   pl.BlockSpec((tm // MOBA_BLOCK, 1, d), lambda i: (i, 0, 0))],
        out_shape=[t_shape, jax.ShapeDtypeStruct((m, d), BF16), t_shape, t_shape,
                   jax.ShapeDtypeStruct((m // MOBA_BLOCK, 1, d), F32)],
        compiler_params=_params("arbitrary"),
        name="moba_proj_t",
    )(x, gamma, w_t, w_k)


def _ret_proj_kernel(x_ref, g_ref, w_ref, cos_ref, sin_ref, q_ref, k_ref, v_ref, gate_ref):
    h = _rms(x_ref[...], g_ref[...]).astype(BF16)
    hk = RET_HEADS * RET_DK
    hv = RET_HEADS * RET_DV
    even = (_iota((1, LANES), 1) % 2) == 0
    for off, scale, out in ((0, 1.0, q_ref), (hk, RET_DK ** -0.5, k_ref)):
        p = _dot(h, w_ref[:, off:off + hk])
        for c in range(hk // LANES):
            x = p[:, c * LANES:(c + 1) * LANES]
            swapped = jnp.where(even, pltpu.roll(x, LANES - 1, 1), pltpu.roll(x, 1, 1))
            tab = pl.ds((c * LANES) % RET_DK, LANES)
            rot = x * cos_ref[:, tab] + swapped * sin_ref[:, tab]
            out[:, c * LANES:(c + 1) * LANES] = (rot * scale).astype(out.dtype)
    v_ref[...] = _dot(h, w_ref[:, 2 * hk:2 * hk + hv]).astype(v_ref.dtype)
    gate_ref[...] = _dot(h, w_ref[:, 2 * hk + hv:])


def _ret_proj(x, gamma, w_all, layer, cos, sin, *, tm, qv_dtype):
    m, d = x.shape
    tm = _row_tile(m, tm)
    hk = RET_HEADS * RET_DK
    hv = RET_HEADS * RET_DV
    n_tab = cos.shape[0] // tm
    assert cos.shape[0] % tm == 0
    row = lambda i: (i, 0)
    return pl.pallas_call(
        _ret_proj_kernel,
        grid=(m // tm,),
        in_specs=[pl.BlockSpec((tm, d), row),
                  pl.BlockSpec((1, d), lambda i: (0, 0)),
                  pl.BlockSpec((None, d, 2 * hk + 2 * hv), lambda i: (layer, 0, 0)),
                  pl.BlockSpec((tm, RET_DK), lambda i: (i % n_tab, 0)),
                  pl.BlockSpec((tm, RET_DK), lambda i: (i % n_tab, 0))],
        out_specs=[pl.BlockSpec((tm, hk), row), pl.BlockSpec((tm, hk), row),
                   pl.BlockSpec((tm, hv), row), pl.BlockSpec((tm, hv), row)],
        out_shape=[jax.ShapeDtypeStruct((m, hk), qv_dtype), jax.ShapeDtypeStruct((m, hk), F32),
                   jax.ShapeDtypeStruct((m, hv), qv_dtype), jax.ShapeDtypeStruct((m, hv), F32)],
        compiler_params=_params("arbitrary"),
        name="ret_proj",
    )(x, gamma, w_all, cos, sin)


def _rope_tables(pos):
    inv = ROPE_BASE ** (-jnp.arange(0, RET_DK, 2, dtype=F32) / RET_DK)
    ang = pos.astype(F32)[:, None] * inv[None, :]
    cos = jnp.repeat(jnp.cos(ang), 2, axis=1)
    sin = jnp.sin(ang)
    return cos, jnp.stack([-sin, sin], axis=-1).reshape(cos.shape)


def _ret_core_kernel(*refs, nb, chunk, n_chunks, has_state, layer, fill_layers):
    refs = list(refs)
    q_ref, k_ref, v_ref, g_ref, dm_ref, qd_ref, kd_ref = refs[:7]
    st0_ref = refs[7] if has_state else None
    o_ref, stout_ref, st_scr = refs[-3:]
    c = pl.program_id(2)

    @pl.when(c == 0)
    def _():
        for s in range(nb):
            st_scr[s] = st0_ref[s, 0] if has_state else jnp.zeros((RET_DK, RET_DV), F32)

    dm = dm_ref[0]
    qd = qd_ref[0]
    kd = kd_ref[0]
    qd_v = jnp.concatenate([qd] * (RET_DV // LANES), axis=1)
    kd_k = jnp.concatenate([kd] * (RET_DK // LANES), axis=1)
    sdec = qd[chunk - 1:chunk, 0:1]
    for s in range(nb):
        q = q_ref[s].astype(BF16)
        kf = k_ref[s]
        v = v_ref[s].astype(BF16)
        st = st_scr[s]
        sc = _dot_nt(q, kf.astype(BF16)) * dm
        inner = _dot(sc.astype(BF16), v)
        cross = _dot(q, st.astype(BF16)) * qd_v
        o = inner + cross
        kv = _dot_tn((kf * kd_k).astype(BF16), v)
        st_new = sdec * st + kv
        st_scr[s] = st_new
        on = o * lax.rsqrt(jnp.mean(o * o, axis=-1, keepdims=True) + EPS)
        gg = g_ref[s]
        o_ref[s] = (on * (gg * jax.nn.sigmoid(gg))).astype(o_ref.dtype)

        @pl.when(c == n_chunks - 1)
        def _():
            if fill_layers:
                for jj in range(fill_layers):
                    stout_ref[jj, s, 0] = st_new if jj == layer else jnp.zeros_like(st_new)
            else:
                stout_ref[s, 0] = st_new


def _ret_core(q, k, v, g, tabs, *, batch, chunk, nb, state_all=None, layer=0, out_all=None):
    hk = RET_HEADS * RET_DK
    hv = RET_HEADS * RET_DV
    s_len = q.shape[0] // batch
    n_chunks = s_len // chunk
    assert s_len % chunk == 0 and batch % nb == 0
    dm, qd, kd = tabs
    seq = lambda b, h, c: (b, c, h)
    tab = lambda b, h, c: (h, 0, 0)
    in_specs = [pl.BlockSpec((nb, chunk, RET_DK), seq), pl.BlockSpec((nb, chunk, RET_DK), seq),
                pl.BlockSpec((nb, chunk, RET_DV), seq), pl.BlockSpec((nb, chunk, RET_DV), seq),
                pl.BlockSpec((1, chunk, chunk), tab), pl.BlockSpec((1, chunk, LANES), tab),
                pl.BlockSpec((1, chunk, LANES), tab)]
    args = [q.reshape(batch, s_len, hk), k.reshape(batch, s_len, hk),
            v.reshape(batch, s_len, hv), g.reshape(batch, s_len, hv), dm, qd, kd]
    aliases = {}
    fill_layers = 0
    if state_all is None:
        st_shape = (batch, RET_HEADS, RET_DK, RET_DV)
        st_spec = pl.BlockSpec((nb, 1, RET_DK, RET_DV), lambda b, h, c: (b, h, 0, 0))
    else:
        n_layers = state_all.shape[0]
        st_shape = state_all.shape
        in_specs.append(pl.BlockSpec((None, nb, 1, RET_DK, RET_DV), lambda b, h, c: (layer, b, h, 0, 0)))
        args.append(state_all)
        if out_all is None:
            fill_layers = n_layers
            st_spec = pl.BlockSpec((n_layers, nb, 1, RET_DK, RET_DV), lambda b, h, c: (0, b, h, 0, 0))
        else:
            in_specs.append(pl.BlockSpec(memory_space=pl.ANY))
            args.append(out_all)
            aliases = {len(args) - 1: 1}
            st_spec = pl.BlockSpec((None, nb, 1, RET_DK, RET_DV), lambda b, h, c: (layer, b, h, 0, 0))
    kern = functools.partial(_ret_core_kernel, nb=nb, chunk=chunk, n_chunks=n_chunks,
                             has_state=state_all is not None, layer=layer, fill_layers=fill_layers)
    if aliases:
        inner = kern
        n_in = len(args)
        kern = lambda *refs: inner(*refs[:n_in - 1], *refs[n_in:])
    o, st = pl.pallas_call(
        kern,
        grid=(batch // nb, RET_HEADS, n_chunks),
        in_specs=in_specs,
        out_specs=[pl.BlockSpec((nb, chunk, RET_DV), seq), st_spec],
        out_shape=[jax.ShapeDtypeStruct((batch, s_len, hv), BF16),
                   jax.ShapeDtypeStruct(st_shape, F32)],
        scratch_shapes=[pltpu.VMEM((nb, RET_DK, RET_DV), F32)],
        input_output_aliases=aliases,
        compiler_params=_params("arbitrary", "arbitrary", "arbitrary"),
        name="ret_core",
    )(*args)
    return o.reshape(batch * s_len, hv), st


def _ret_prompt_kernel(q_ref, k_ref, v_ref, g_ref, dm_ref, qd_ref, kd_ref, o_ref, stout_ref, st_scr, *,
                       chunk, n_chunks):
    c = pl.program_id(1)
    dk, dv = RET_DK, RET_DV

    @pl.when(c == 0)
    def _():
        st_scr[...] = jnp.zeros((RET_HEADS, dk, dv), F32)

    for h in range(RET_HEADS):
        q = q_ref[0, :, h * dk:(h + 1) * dk]
        kf = k_ref[0, :, h * dk:(h + 1) * dk]
        v = v_ref[0, :, h * dv:(h + 1) * dv]
        st = st_scr[h]
        qd = qd_ref[h]
        kd_k = jnp.concatenate([kd_ref[h]] * (dk // LANES), axis=1)
        sc = _dot_nt(q, kf.astype(BF16))
        cross = _dot(q, st.astype(BF16))
        kv = _dot_tn((kf * kd_k).astype(BF16), v)
        inner = _dot((sc * dm_ref[h]).astype(BF16), v)
        o = inner + cross * jnp.concatenate([qd] * (dv // LANES), axis=1)
        st_scr[h] = qd[chunk - 1:chunk, 0:1] * st + kv
        on = o * lax.rsqrt(jnp.mean(o * o, axis=-1, keepdims=True) + EPS)
        gg = g_ref[0, :, h * dv:(h + 1) * dv]
        o_ref[0, :, h * dv:(h + 1) * dv] = (on * (gg * jax.nn.sigmoid(gg))).astype(o_ref.dtype)

    @pl.when(c == n_chunks - 1)
    def _():
        stout_ref[0] = st_scr[...]


def _ret_prompt(q, k, v, g, tabs, *, batch, chunk):
    hk = RET_HEADS * RET_DK
    hv = RET_HEADS * RET_DV
    s_len = q.shape[0] // batch
    n_chunks = s_len // chunk
    assert s_len % chunk == 0
    dm, qd, kd = tabs
    seq = lambda b, c: (b, c, 0)
    tab = lambda b, c: (0, 0, 0)
    o, st = pl.pallas_call(
        functools.partial(_ret_prompt_kernel, chunk=chunk, n_chunks=n_chunks),
        grid=(batch, n_chunks),
        in_specs=[pl.BlockSpec((1, chunk, hk), seq), pl.BlockSpec((1, chunk, hk), seq),
                  pl.BlockSpec((1, chunk, hv), seq), pl.BlockSpec((1, chunk, hv), seq),
                  pl.BlockSpec((RET_HEADS, chunk, chunk), tab), pl.BlockSpec((RET_HEADS, chunk, LANES), tab),
                  pl.BlockSpec((RET_HEADS, chunk, LANES), tab)],
        out_specs=[pl.BlockSpec((1, chunk, hv), seq),
                   pl.BlockSpec((1, RET_HEADS, RET_DK, RET_DV), lambda b, c: (b, 0, 0, 0))],
        out_shape=[jax.ShapeDtypeStruct((batch, s_len, hv), BF16),
                   jax.ShapeDtypeStruct((batch, RET_HEADS, RET_DK, RET_DV), F32)],
        scratch_shapes=[pltpu.VMEM((RET_HEADS, RET_DK, RET_DV), F32)],
        compiler_params=_params("arbitrary", "arbitrary"),
        name="ret_prompt",
    )(q.reshape(batch, s_len, hk), k.reshape(batch, s_len, hk), v.reshape(batch, s_len, hv),
      g.reshape(batch, s_len, hv), dm, qd, kd)
    return o.reshape(batch * s_len, hv), st


def _ret_tables(chunk):
    log_g = jnp.log(1.0 - 2.0 ** (-5.0 - jnp.arange(RET_HEADS, dtype=F32)))
    i = jnp.arange(chunk, dtype=F32)
    diff = i[:, None] - i[None, :]
    dm = jnp.where(diff >= 0, jnp.exp(log_g[:, None, None] * jnp.maximum(diff, 0.0)), 0.0)
    qd = jnp.exp(log_g[:, None] * (i[None, :] + 1.0))
    kd = jnp.exp(log_g[:, None] * (chunk - 1.0 - i[None, :]))
    rep = lambda t: jnp.broadcast_to(t[:, :, None], (RET_HEADS, chunk, LANES))
    return dm, rep(qd), rep(kd)


def _mlp_kernel(*refs, tf, has_proj, has_final):
    refs = list(refs)
    x_ref = refs.pop(0)
    a_ref, wo_ref = (refs.pop(0), refs.pop(0)) if has_proj else (None, None)
    g_ref, wu_ref, wd_ref = refs.pop(0), refs.pop(0), refs.pop(0)
    gf_ref = refs.pop(0) if has_final else None
    o_ref, = refs
    x = x_ref[...]
    if has_proj:
        x = x + _dot(a_ref[...].astype(BF16), wo_ref[...])
    h = _rms(x, g_ref[...]).astype(BF16)
    acc = None
    for f in range(D_FF // tf):
        a = jnp.maximum(_dot(h, wu_ref[:, f * tf:(f + 1) * tf]), 0.0)
        part = _dot((a * a).astype(BF16), wd_ref[f * tf:(f + 1) * tf, :])
        acc = part if acc is None else acc + part
    y = x + acc
    o_ref[...] = _rms(y, gf_ref[...]) if has_final else y


def _layer_weight(w_all, layer):
    return pl.BlockSpec((None,) + w_all.shape[1:], lambda i: (layer, 0, 0), pipeline_mode=pl.Buffered(1))


def _mlp(x, gamma, w_up_all, w_down_all, layer, *, tm, tf, a=None, w_o_all=None, w_o_layer=0, final_gamma=None):
    m, d = x.shape
    tm = _row_tile(m, tm)
    row = lambda i: (i, 0)
    const = lambda i: (0, 0)
    in_specs = [pl.BlockSpec((tm, d), row)]
    args = [x]
    if a is not None:
        in_specs += [pl.BlockSpec((tm, a.shape[1]), row), _layer_weight(w_o_all, w_o_layer)]
        args += [a, w_o_all]
    in_specs += [pl.BlockSpec((1, d), const), _layer_weight(w_up_all, layer), _layer_weight(w_down_all, layer)]
    args += [gamma, w_up_all, w_down_all]
    if final_gamma is not None:
        in_specs.append(pl.BlockSpec((1, d), const))
        args.append(final_gamma)
    return pl.pallas_call(
        functools.partial(_mlp_kernel, tf=tf, has_proj=a is not None, has_final=final_gamma is not None),
        grid=(m // tm,),
        in_specs=in_specs,
        out_specs=pl.BlockSpec((tm, d), row),
        out_shape=jax.ShapeDtypeStruct((m, d), F32),
        compiler_params=_params("arbitrary"),
        name="mlp",
    )(*args)


def _conv_kernel(*refs, tiles_per_seq, seq_in_tile):
    if seq_in_tile:
        x_ref, g_ref, win_ref, wc_ref, wout_ref, p1_ref, p2_ref, o_ref, u_ref = refs
    else:
        x_ref, g_ref, win_ref, wc_ref, wout_ref, o_ref, u_ref, carry_scr = refs
    d = D_MODEL
    x = x_ref[...]
    tm = x.shape[0]
    h = _rms(x, g_ref[...]).astype(BF16)
    bg = _dot(h, win_ref[:, 0:d])
    u = _dot(h, win_ref[:, d:2 * d]) * _dot(h, win_ref[:, 2 * d:3 * d])
    row = _iota((tm, 1), 0)
    r1 = pltpu.roll(u, 1, 0)
    r2 = pltpu.roll(u, 2, 0)
    if seq_in_tile:
        t = row % seq_in_tile
        um1 = jnp.where(t >= 1, r1, 0.0) + p1_ref[...]
        um2 = jnp.where(t >= 2, r2, 0.0) + p2_ref[...]
        u_ref[...] = u
    else:
        @pl.when((pl.program_id(0) % tiles_per_seq) == 0)
        def _():
            carry_scr[...] = jnp.zeros((SUBLANES, d), F32)

        carry = carry_scr[...]
        c1 = carry[SUBLANES - 1:SUBLANES]
        c2 = carry[SUBLANES - 2:SUBLANES - 1]
        um1 = jnp.where(row >= 1, r1, c1)
        um2 = jnp.where(row >= 2, r2, jnp.where(row == 1, c1, c2))
        tail = u[tm - SUBLANES:tm]
        carry_scr[...] = tail
        u_ref[...] = tail
    wc = wc_ref[...]
    y = wc[0:1] * um2 + wc[1:2] * um1 + wc[2:3] * u
    o_ref[...] = x + _dot((bg * y).astype(BF16), wout_ref[...])


def _conv_mixer(x, gamma, w_in, w_c, w_out, *, tm, seq_len, state=None):
    m, d = x.shape
    n_seq = m // seq_len
    wc_pad = jnp.zeros((SUBLANES, d), F32).at[:CONV_WIDTH].set(w_c)
    row = lambda i: (i, 0)
    const = lambda i: (0, 0)
    in_specs = [pl.BlockSpec((tm, d), row), pl.BlockSpec((1, d), const),
                pl.BlockSpec((d, 3 * d), const), pl.BlockSpec((SUBLANES, d), const),
                pl.BlockSpec((d, d), const)]
    args = [x, gamma, w_in, wc_pad, w_out]
    if state is not None:
        tm = _row_tile(m, tm)
        assert tm % seq_len == 0 and seq_len >= CONV_WIDTH - 1
        zeros = jnp.zeros((n_seq, seq_len, d), F32)
        p1 = zeros.at[:, 0].set(state[:, 1]).reshape(m, d)
        p2 = zeros.at[:, 0].set(state[:, 0]).at[:, 1].set(state[:, 1]).reshape(m, d)
        in_specs += [pl.BlockSpec((tm, d), row), pl.BlockSpec((tm, d), row)]
        args += [p1, p2]
        u_rows, u_blk, scratch = m, tm, []
        kern = functools.partial(_conv_kernel, tiles_per_seq=0, seq_in_tile=seq_len)
    else:
        tm = _row_tile(seq_len, tm)
        u_rows, u_blk = (m // tm) * SUBLANES, SUBLANES
        scratch = [pltpu.VMEM((SUBLANES, d), F32)]
        kern = functools.partial(_conv_kernel, tiles_per_seq=seq_len // tm, seq_in_tile=0)
    out, u = pl.pallas_call(
        kern,
        grid=(m // tm,),
        in_specs=in_specs,
        out_specs=[pl.BlockSpec((tm, d), row), pl.BlockSpec((u_blk, d), row)],
        out_shape=[jax.ShapeDtypeStruct((m, d), F32), jax.ShapeDtypeStruct((u_rows, d), F32)],
        scratch_shapes=scratch,
        compiler_params=_params("arbitrary"),
        name="conv_mixer",
    )(*args)
    u_last = u.reshape(n_seq, -1, d)[:, -(CONV_WIDTH - 1):]
    return out, u_last


def _topk_bias(gate, blk, valid, axis):
    n = gate.shape[axis]
    g = jnp.where(valid, gate, NEG_INF)
    sel = jnp.zeros(gate.shape, jnp.bool_)
    for _ in range(MOBA_TOPK):
        mx = jnp.max(g, axis=axis, keepdims=True)
        idx = jnp.min(jnp.where(g == mx, blk, n), axis=axis, keepdims=True)
        pick = blk == idx
        sel = sel | (pick & valid)
        g = jnp.where(pick, NEG_INF, g)
    return jnp.where(sel, 0.0, NEG_INF)


def _moba_prompt_kernel(qt_ref, k_ref, vt_ref, km_ref, o_ref, bias_scr, *, nblk, n_tiles):
    i = pl.program_id(2)
    bs = MOBA_BLOCK
    hd = MOBA_HD
    per_tile = LANES // hd
    heads = [(t, h2) for t in range(n_tiles) for h2 in range(per_tile)]
    feat_row = _iota((LANES, 1), 0) // hd
    blk = _iota((nblk, 1), 0)
    causal = _iota((bs, 1), 0) <= _iota((1, bs), 1)
    valid = blk < i

    def key_block(j):
        start = pl.multiple_of(j * bs, bs)
        return k_ref[pl.ds(start, bs), :], vt_ref[0, :, pl.ds(start, bs)].astype(BF16)

    def scores(kj, qbs):
        return [_dot(kj[:, t * LANES:(t + 1) * LANES], qb) for (t, _), qb in zip(heads, qbs)]

    def values(vtj, ps):
        return [_dot(vtj[n * hd:(n + 1) * hd], p.astype(BF16)) for n, p in enumerate(ps)]

    qbs = []
    for n, (t, h2) in enumerate(heads):
        qt = qt_ref[0, t * LANES:(t + 1) * LANES, :]
        km = km_ref[:, t * LANES:(t + 1) * LANES]
        qb = jnp.where(feat_row == h2, qt * (hd ** -0.5 * LOG2_E), 0.0).astype(BF16)
        qbs.append(qb)
        bias_scr[n] = _topk_bias(_dot(km.astype(BF16), qb), blk, valid, 0)

    k_own, vt_own = key_block(i)
    ss = [jnp.where(causal, s, NEG_INF) for s in scores(k_own, qbs)]
    ms = [jnp.max(s, axis=0, keepdims=True) for s in ss]
    ps = [jnp.exp2(s - m) for s, m in zip(ss, ms)]
    ls = [jnp.sum(p, axis=0, keepdims=True) for p in ps]
    accs = values(vt_own, ps)

    def body(jj, carry):
        ms, ls, accs = carry
        blocks = [key_block(2 * jj + u) for u in range(2)]
        ss = [scores(blocks[u][0], qbs) for u in range(2)]
        biases = [[bias_scr[n, pl.ds(2 * jj + u, 1), :] for n in range(len(heads))] for u in range(2)]
        tops = [[jnp.max(s, axis=0, keepdims=True) + bias for s, bias in zip(ss[u], biases[u])] for u in range(2)]
        m_new = [jnp.maximum(m, jnp.maximum(t0, t1)) for m, t0, t1 in zip(ms, tops[0], tops[1])]
        alphas = [jnp.exp2(m - mn) for m, mn in zip(ms, m_new)]
        ps = [[jnp.exp2(s - (mn - bias)) for s, mn, bias in zip(ss[u], m_new, biases[u])] for u in range(2)]
        ls = [a * l + jnp.sum(p0, axis=0, keepdims=True) + jnp.sum(p1, axis=0, keepdims=True)
              for a, l, p0, p1 in zip(alphas, ls, ps[0], ps[1])]
        pvs = [values(blocks[u][1], ps[u]) for u in range(2)]
        accs = [a * acc + pv0 + pv1 for a, acc, pv0, pv1 in zip(alphas, accs, pvs[0], pvs[1])]
        return m_new, ls, accs

    ms, ls, accs = lax.fori_loop(0, (i + 1) // 2, body, (ms, ls, accs))
    ot = jnp.concatenate([acc / l for acc, l in zip(accs, ls)], axis=0)
    o_ref[...] = ot.T.astype(o_ref.dtype)


def _moba_prompt(qt, k, vt, kmean):
    batch, d, s_len = qt.shape
    nblk = s_len // MOBA_BLOCK
    assert s_len % MOBA_BLOCK == 0 and nblk % SUBLANES == 0
    n_tiles = MOBA_TILES_PER_STEP
    width = n_tiles * LANES
    km2 = kmean.reshape(batch * nblk, d)
    return pl.pallas_call(
        functools.partial(_moba_prompt_kernel, nblk=nblk, n_tiles=n_tiles),
        grid=(batch, d // width, nblk),
        in_specs=[pl.BlockSpec((1, width, MOBA_BLOCK), lambda b, h, i: (b, h, i)),
                  pl.BlockSpec((s_len, width), lambda b, h, i: (b, h)),
                  pl.BlockSpec((1, width, s_len), lambda b, h, i: (b, h, 0)),
                  pl.BlockSpec((nblk, width), lambda b, h, i: (b, h))],
        out_specs=pl.BlockSpec((MOBA_BLOCK, width), lambda b, h, i: (b * nblk + i, h)),
        out_shape=jax.ShapeDtypeStruct((batch * s_len, d), BF16),
        scratch_shapes=[pltpu.VMEM((width // MOBA_HD, nblk, MOBA_BLOCK), F32)],
        compiler_params=_params("arbitrary", "arbitrary", "arbitrary"),
        name="moba_prompt",
    )(qt, k, vt, km2)


def _moba_paged_kernel(pt_ref, q_ref, kn_ref, vn_ref, *refs, n_pages, page, t_new):
    k_refs = refs[0:n_pages]
    v_refs = refs[n_pages:2 * n_pages]
    o_ref = refs[2 * n_pages]
    qbd_scr, s_scr, bias_scr, m_scr = refs[2 * n_pages + 1:]
    del pt_ref
    ph = pl.program_id(1)
    ppb = MOBA_BLOCK // page
    n_blk = n_pages // ppb
    n_col = MOBA_HEADS * t_new
    d = D_MODEL
    blk_lane = _iota((1, LANES), 1)
    own_ok = _iota((n_col, t_new), 1) <= _iota((n_col, t_new), 0) % t_new

    def own_scores(qbd):
        return jnp.where(own_ok, _dot_nt(qbd, kn_ref[0].astype(BF16)), NEG_INF)

    def block_bias(bias, p):
        b = p // ppb
        return bias[:, b:b + 1]

    @pl.when(ph == 0)
    def _():
        q = q_ref[0]
        qt = jnp.concatenate([q] * MOBA_HEADS, axis=0)
        bd = _iota((n_col, d), 0) // t_new == _iota((n_col, d), 1) // MOBA_HD
        qbd = jnp.where(bd, qt * (MOBA_HD ** -0.5), 0.0).astype(BF16)
        qbd_scr[...] = qbd
        gate = jnp.zeros((n_col, LANES), F32)
        for b in range(n_blk):
            ssum = None
            for r in range(ppb):
                sc = _dot(qbd, k_refs[b * ppb + r][0].astype(BF16))
                s_scr[b * ppb + r] = sc
                ssum = sc if ssum is None else ssum + sc
            gate = jnp.where(blk_lane == b, jnp.sum(ssum, axis=1, keepdims=True), gate)
        bias = _topk_bias(gate, blk_lane, blk_lane < n_blk, 1)
        bias_scr[...] = bias
        top = None
        for p in range(n_pages):
            sb = s_scr[p] + block_bias(bias, p)
            top = sb if top is None else jnp.maximum(top, sb)
        m = jnp.maximum(jnp.max(own_scores(qbd), axis=1, keepdims=True), jnp.max(top, axis=1, keepdims=True))
        m_scr[...] = jnp.broadcast_to(m, (n_col, LANES))

    @pl.when(ph == 1)
    def _():
        bias = bias_scr[...]
        m = m_scr[:, 0:1]
        e_own = jnp.exp(own_scores(qbd_scr[...]) - m)
        esum = None
        for p in range(n_pages):
            e = jnp.exp(s_scr[p] + block_bias(bias, p) - m)
            s_scr[p] = e
            esum = e if esum is None else esum + e
        inv = 1.0 / (jnp.sum(e_own, axis=1, keepdims=True) + jnp.sum(esum, axis=1, keepdims=True))
        acc = _dot((e_own * inv).astype(BF16), vn_ref[0].astype(BF16))
        for p in range(n_pages):
            acc = acc + _dot_nt((s_scr[p] * inv).astype(BF16), v_refs[p][0].astype(BF16))
        bd = _iota((n_col, d), 0) // t_new == _iota((n_col, d), 1) // MOBA_HD
        a = jnp.where(bd, acc, 0.0)
        out = a[0:t_new]
        for hd in range(1, MOBA_HEADS):
            out = out + a[hd * t_new:(hd + 1) * t_new]
        o_ref[0] = out


def _moba_paged(q, k_new, v_new, cache_kt, cache_vt, page_table):
    db, t_new, d = q.shape
    n_pages = page_table.shape[1]
    page = cache_kt.shape[2]
    past = n_pages * page
    assert past % MOBA_BLOCK == 0 and MOBA_BLOCK % page == 0
    assert t_new <= MOBA_BLOCK and MOBA_TOPK <= past // MOBA_BLOCK <= LANES
    n_col = MOBA_HEADS * t_new
    pt = page_table.reshape(-1).astype(jnp.int32)

    def k_map(r):
        return lambda b, ph, pt_ref: (pt_ref[b * n_pages + r], 0, 0)

    def v_map(r):
        return lambda b, ph, pt_ref: (pt_ref[jnp.where(ph == 0, jnp.maximum(b - 1, 0), b) * n_pages + r], 0, 0)

    tok = pl.BlockSpec((1, t_new, d), lambda b, ph, pt_ref: (b, 0, 0))
    in_specs = [tok, tok, tok]
    in_specs += [pl.BlockSpec((1, d, page), k_map(r)) for r in range(n_pages)]
    in_specs += [pl.BlockSpec((1, d, page), v_map(r)) for r in range(n_pages)]
    grid_spec = pltpu.PrefetchScalarGridSpec(
        num_scalar_prefetch=1,
        grid=(db, 2),
        in_specs=in_specs,
        out_specs=pl.BlockSpec((1, t_new, d), lambda b, ph, pt_ref: (b, 0, 0)),
        scratch_shapes=[pltpu.VMEM((n_col, d), BF16), pltpu.VMEM((n_pages, n_col, page), F32),
                        pltpu.VMEM((n_col, LANES), F32), pltpu.VMEM((n_col, LANES), F32)],
    )
    return pl.pallas_call(
        functools.partial(_moba_paged_kernel, n_pages=n_pages, page=page, t_new=t_new),
        grid_spec=grid_spec,
        out_shape=jax.ShapeDtypeStruct((db, t_new, d), F32),
        compiler_params=_params("arbitrary", "arbitrary"),
        name="moba_paged",
    )(pt, q, k_new, v_new, *([cache_kt] * n_pages), *([cache_vt] * n_pages))


def _feature_major_pages(cache):
    n_phys, page, n_h, hd = cache.shape
    return jnp.transpose(cache, (0, 2, 3, 1)).reshape(n_phys, n_h * hd, page)


def _token_rows(xt, batch):
    _, _, s_len = xt.shape
    return jnp.transpose(xt.reshape(batch, MOBA_HEADS, MOBA_HD, s_len), (0, 3, 1, 2))


def kernel(x_prompt, x_sample, state_ret, cache_k, cache_v, state_conv, page_table, norm_mix, norm_ff, norm_final, ret_w_qkvg, ret_w_o, moba_w_qkv, moba_w_o, conv_w_in, conv_w, conv_w_out, mlp_w_up, mlp_w_down):
    b, s, d = x_prompt.shape
    db, t, _ = x_sample.shape
    depth = norm_mix.shape[0]
    n_pages = page_table.shape[1]
    page = cache_k.shape[2]
    past_len = n_pages * page

    xp = x_prompt.reshape(b * s, d)
    xs = x_sample.reshape(db * t, d)
    tm_p = TM_PROMPT
    tm_s = db * t

    cos_p, sin_p = _rope_tables(jnp.arange(s, dtype=jnp.int32))
    cos_s, sin_s = _rope_tables(past_len + jnp.arange(t, dtype=jnp.int32))
    cos_s = jnp.tile(cos_s, (tm_s // t, 1))
    sin_s = jnp.tile(sin_s, (tm_s // t, 1))
    tabs_p = _ret_tables(RET_CHUNK)
    tabs_s = _ret_tables(t)

    ret_w = ret_w_qkvg.astype(BF16)
    ret_wo = ret_w_o.astype(BF16)
    moba_wo = moba_w_o.astype(BF16)
    up_w = mlp_w_up.astype(BF16)
    down_w = mlp_w_down.astype(BF16)

    ret_p, ret_s_all, kp_l, vp_l, ks_l, vs_l, conv_p, conv_s = [], None, [], [], [], [], [], []
    for i in range(depth):
        kind = i % N_MIXERS
        j = i // N_MIXERS
        g_mix = norm_mix[i].reshape(1, d)
        g_ff = norm_ff[i].reshape(1, d)
        if kind == 0:
            wo_all = ret_wo
            q, k, v, g = _ret_proj(xp, g_mix, ret_w, j, cos_p, sin_p, tm=tm_p, qv_dtype=BF16)
            op, st_p = _ret_prompt(q, k, v, g, tabs_p, batch=b, chunk=RET_CHUNK)
            q, k, v, g = _ret_proj(xs, g_mix, ret_w, j, cos_s, sin_s, tm=tm_s, qv_dtype=F32)
            os_, ret_s_all = _ret_core(q, k, v, g, tabs_s, batch=db, chunk=t, nb=RET_SEQS_PER_STEP,
                                       state_all=state_ret, layer=j, out_all=ret_s_all)
            ret_p.append(st_p)
        elif kind == 1:
            wo_all = moba_wo
            qt, k, kt, vt, kmean = _moba_proj_t(xp, g_mix, moba_w_qkv[j], batch=b, tm=tm_p)
            op = _moba_prompt(qt, k, vt, kmean)
            kp_l.append(_token_rows(kt, b))
            vp_l.append(_token_rows(vt, b))
            q, k, v = _norm_proj(xs, g_mix, moba_w_qkv[j].astype(BF16), (d, d, d), (F32, F32, F32), tm=tm_s)
            os_ = _moba_paged(q.reshape(db, t, d), k.reshape(db, t, d), v.reshape(db, t, d),
                              _feature_major_pages(cache_k[j]), _feature_major_pages(cache_v[j]),
                              page_table).reshape(db * t, d)
            ks_l.append(k.reshape(db, t, MOBA_HEADS, MOBA_HD))
            vs_l.append(v.reshape(db, t, MOBA_HEADS, MOBA_HD))
        else:
            w_in = conv_w_in[j].astype(BF16)
            w_out = conv_w_out[j].astype(BF16)
            xp, buf_p = _conv_mixer(xp, g_mix, w_in, conv_w[j], w_out, tm=tm_p, seq_len=s)
            xs, buf_s = _conv_mixer(xs, g_mix, w_in, conv_w[j], w_out, tm=tm_s, seq_len=t,
                                    state=state_conv[j])
            conv_p.append(buf_p)
            conv_s.append(buf_s)
            op = os_ = wo_all = None
        g_fin = norm_final.reshape(1, d) if i == depth - 1 else None
        xp = _mlp(xp, g_ff, up_w, down_w, i, tm=TM_MLP, tf=TF_MLP, a=op, w_o_all=wo_all, w_o_layer=j,
                  final_gamma=g_fin)
        xs = _mlp(xs, g_ff, up_w, down_w, i, tm=tm_s, tf=TF_MLP, a=os_, w_o_all=wo_all, w_o_layer=j,
                  final_gamma=g_fin)
    y_prompt = xp.reshape(b, s, d)
    y_sample = xs.reshape(db, t, d)
    return (y_prompt, y_sample, jnp.stack(ret_p), ret_s_all, jnp.stack(kp_l), jnp.stack(vp_l),
            jnp.stack(ks_l), jnp.stack(vs_l), jnp.stack(conv_p), jnp.stack(conv_s))
```

```python
import functools

import jax
import jax.numpy as jnp
from jax import lax
from jax.experimental import pallas as pl
from jax.experimental.pallas import tpu as pltpu

F32 = jnp.float32
BF16 = jnp.bfloat16
NEG_INF = float("-inf")

D_MODEL = 1024
N_MIXERS = 3
RET_HEADS = 4
RET_DK = D_MODEL // RET_HEADS
RET_DV = 2 * RET_DK
ROPE_BASE = 10000.0
MOBA_HEADS = 16
MOBA_HD = D_MODEL // MOBA_HEADS
MOBA_BLOCK = 256
MOBA_TOPK = 3
CONV_WIDTH = 3
D_FF = 4 * D_MODEL
EPS = 1e-6
LOG2_E = 1.4426950408889634

LANES = 128
SUBLANES = 8

TM_PROMPT = 512
TM_MLP = 512
TF_MLP = 1024
RET_CHUNK = 256
RET_SEQS_PER_STEP = 8
MOBA_TILES_PER_STEP = 8


def _params(*sem):
    return pltpu.CompilerParams(dimension_semantics=sem)


def _row_tile(m, want):
    t = min(m, want)
    assert m % t == 0, (m, t)
    return t


def _rms(x, gamma):
    ms = jnp.mean(x * x, axis=-1, keepdims=True)
    return x * lax.rsqrt(ms + EPS) * gamma


def _dot(a, b):
    return jnp.dot(a, b, preferred_element_type=F32)


def _dot_nt(a, b):
    return lax.dot_general(a, b, (((1,), (1,)), ((), ())), preferred_element_type=F32)


def _dot_tn(a, b):
    return lax.dot_general(a, b, (((0,), (0,)), ((), ())), preferred_element_type=F32)


def _iota(shape, dim):
    return lax.broadcasted_iota(jnp.int32, shape, dim)


def _norm_proj_kernel(x_ref, g_ref, w_ref, *out_refs, widths):
    h = _rms(x_ref[...], g_ref[...]).astype(BF16)
    off = 0
    for idx, wd in enumerate(widths):
        out_refs[idx][...] = _dot(h, w_ref[:, off:off + wd]).astype(out_refs[idx].dtype)
        off += wd


def _norm_proj(x, gamma, w, widths, dtypes, *, tm):
    m, d = x.shape
    tm = _row_tile(m, tm)
    n = sum(widths)
    return pl.pallas_call(
        functools.partial(_norm_proj_kernel, widths=tuple(widths)),
        grid=(m // tm,),
        in_specs=[pl.BlockSpec((tm, d), lambda i: (i, 0)),
                  pl.BlockSpec((1, d), lambda i: (0, 0)),
                  pl.BlockSpec((d, n), lambda i: (0, 0))],
        out_specs=[pl.BlockSpec((tm, wd), lambda i: (i, 0)) for wd in widths],
        out_shape=[jax.ShapeDtypeStruct((m, wd), dt) for wd, dt in zip(widths, dtypes)],
        compiler_params=_params("arbitrary"),
        name="norm_proj",
    )(x, gamma, w)


def _moba_proj_t_kernel(x_ref, g_ref, wt_ref, qt_ref, k_ref, kt_ref, vt_ref, vtb_ref, km_ref):
    d = D_MODEL
    h = _rms(x_ref[...], g_ref[...]).astype(BF16)
    qt_ref[0] = _dot_nt(wt_ref[0:d, :], h)
    kt = _dot_nt(wt_ref[d:2 * d, :], h)
    kt_ref[0] = kt
    vt = _dot_nt(wt_ref[2 * d:3 * d, :], h)
    vt_ref[0] = vt
    vtb_ref[0] = vt.astype(BF16)
    k = kt.T
    k_ref[...] = k.astype(BF16)
    for r in range(k.shape[0] // MOBA_BLOCK):
        km_ref[r] = jnp.mean(k[r * MOBA_BLOCK:(r + 1) * MOBA_BLOCK], axis=0, keepdims=True)


def _moba_proj_t(x, gamma, w_qkv, *, batch, tm):
    m, d = x.shape
    s_len = m // batch
    tm = _row_tile(s_len, tm)
    assert tm % MOBA_BLOCK == 0
    per_seq = s_len // tm
    w_t = w_qkv.T.astype(BF16)
    feat = lambda i: (i // per_seq, 0, i % per_seq)
    t_shape = jax.ShapeDtypeStruct((batch, d, s_len), F32)
    return pl.pallas_call(
        _moba_proj_t_kernel,
        grid=(m // tm,),
        in_specs=[pl.BlockSpec((tm, d), lambda i: (i, 0)),
                  pl.BlockSpec((1, d), lambda i: (0, 0)),
                  pl.BlockSpec((3 * d, d), lambda i: (0, 0))],
        out_specs=[pl.BlockSpec((1, d, tm), feat),
                   pl.BlockSpec((tm, d), lambda i: (i, 0)),
                   pl.BlockSpec((1, d, tm), feat),
                   pl.BlockSpec((1, d, tm), feat),
                   pl.BlockSpec((1, d, tm), feat),
                   pl.BlockSpec((tm // MOBA_BLOCK, 1, d), lambda i: (i, 0, 0))],
        out_shape=[t_shape, jax.ShapeDtypeStruct((m, d), BF16), t_shape, t_shape,
                   jax.ShapeDtypeStruct((batch, d, s_len), BF16),
                   jax.ShapeDtypeStruct((m // MOBA_BLOCK, 1, d), F32)],
        compiler_params=_params("arbitrary"),
        name="moba_proj_t",
    )(x, gamma, w_t)


def _ret_proj_kernel(x_ref, g_ref, w_ref, cos_ref, sin_ref, q_ref, k_ref, v_ref, gate_ref):
    h = _rms(x_ref[...], g_ref[...]).astype(BF16)
    hk = RET_HEADS * RET_DK
    hv = RET_HEADS * RET_DV
    even = (_iota((1, LANES), 1) % 2) == 0
    for off, scale, out in ((0, 1.0, q_ref), (hk, RET_DK ** -0.5, k_ref)):
        p = _dot(h, w_ref[:, off:off + hk])
        for c in range(hk // LANES):
            x = p[:, c * LANES:(c + 1) * LANES]
            swapped = jnp.where(even, pltpu.roll(x, LANES - 1, 1), pltpu.roll(x, 1, 1))
            tab = pl.ds((c * LANES) % RET_DK, LANES)
            rot = x * cos_ref[:, tab] + swapped * sin_ref[:, tab]
            out[:, c * LANES:(c + 1) * LANES] = (rot * scale).astype(out.dtype)
    v_ref[...] = _dot(h, w_ref[:, 2 * hk:2 * hk + hv]).astype(v_ref.dtype)
    gate_ref[...] = _dot(h, w_ref[:, 2 * hk + hv:])


def _ret_proj(x, gamma, w_all, layer, cos, sin, *, tm, qv_dtype):
    m, d = x.shape
    tm = _row_tile(m, tm)
    hk = RET_HEADS * RET_DK
    hv = RET_HEADS * RET_DV
    n_tab = cos.shape[0] // tm
    assert cos.shape[0] % tm == 0
    row = lambda i: (i, 0)
    return pl.pallas_call(
        _ret_proj_kernel,
        grid=(m // tm,),
        in_specs=[pl.BlockSpec((tm, d), row),
                  pl.BlockSpec((1, d), lambda i: (0, 0)),
                  pl.BlockSpec((None, d, 2 * hk + 2 * hv), lambda i: (layer, 0, 0)),
                  pl.BlockSpec((tm, RET_DK), lambda i: (i % n_tab, 0)),
                  pl.BlockSpec((tm, RET_DK), lambda i: (i % n_tab, 0))],
        out_specs=[pl.BlockSpec((tm, hk), row), pl.BlockSpec((tm, hk), row),
                   pl.BlockSpec((tm, hv), row), pl.BlockSpec((tm, hv), row)],
        out_shape=[jax.ShapeDtypeStruct((m, hk), qv_dtype), jax.ShapeDtypeStruct((m, hk), F32),
                   jax.ShapeDtypeStruct((m, hv), qv_dtype), jax.ShapeDtypeStruct((m, hv), F32)],
        compiler_params=_params("arbitrary"),
        name="ret_proj",
    )(x, gamma, w_all, cos, sin)


def _rope_tables(pos):
    inv = ROPE_BASE ** (-jnp.arange(0, RET_DK, 2, dtype=F32) / RET_DK)
    ang = pos.astype(F32)[:, None] * inv[None, :]
    cos = jnp.repeat(jnp.cos(ang), 2, axis=1)
    sin = jnp.sin(ang)
    return cos, jnp.stack([-sin, sin], axis=-1).reshape(cos.shape)


def _ret_core_kernel(*refs, nb, chunk, n_chunks, has_state, layer, fill_layers):
    refs = list(refs)
    q_ref, k_ref, v_ref, g_ref, dm_ref, qd_ref, kd_ref = refs[:7]
    st0_ref = refs[7] if has_state else None
    o_ref, stout_ref, st_scr = refs[-3:]
    c = pl.program_id(2)

    @pl.when(c == 0)
    def _():
        for s in range(nb):
            st_scr[s] = st0_ref[s, 0] if has_state else jnp.zeros((RET_DK, RET_DV), F32)

    dm = dm_ref[0]
    qd = qd_ref[0]
    kd = kd_ref[0]
    qd_v = jnp.concatenate([qd] * (RET_DV // LANES), axis=1)
    kd_k = jnp.concatenate([kd] * (RET_DK // LANES), axis=1)
    sdec = qd[chunk - 1:chunk, 0:1]
    for s in range(nb):
        q = q_ref[s].astype(BF16)
        kf = k_ref[s]
        v = v_ref[s].astype(BF16)
        st = st_scr[s]
        sc = _dot_nt(q, kf.astype(BF16)) * dm
        inner = _dot(sc.astype(BF16), v)
        cross = _dot(q, st.astype(BF16)) * qd_v
        o = inner + cross
        kv = _dot_tn((kf * kd_k).astype(BF16), v)
        st_new = sdec * st + kv
        st_scr[s] = st_new
        on = o * lax.rsqrt(jnp.mean(o * o, axis=-1, keepdims=True) + EPS)
        gg = g_ref[s]
        o_ref[s] = (on * (gg * jax.nn.sigmoid(gg))).astype(o_ref.dtype)

        @pl.when(c == n_chunks - 1)
        def _():
            if fill_layers:
                for jj in range(fill_layers):
                    stout_ref[jj, s, 0] = st_new if jj == layer else jnp.zeros_like(st_new)
            else:
                stout_ref[s, 0] = st_new


def _ret_core(q, k, v, g, tabs, *, batch, chunk, nb, state_all=None, layer=0, out_all=None):
    hk = RET_HEADS * RET_DK
    hv = RET_HEADS * RET_DV
    s_len = q.shape[0] // batch
    n_chunks = s_len // chunk
    assert s_len % chunk == 0 and batch % nb == 0
    dm, qd, kd = tabs
    seq = lambda b, h, c: (b, c, h)
    tab = lambda b, h, c: (h, 0, 0)
    in_specs = [pl.BlockSpec((nb, chunk, RET_DK), seq), pl.BlockSpec((nb, chunk, RET_DK), seq),
                pl.BlockSpec((nb, chunk, RET_DV), seq), pl.BlockSpec((nb, chunk, RET_DV), seq),
                pl.BlockSpec((1, chunk, chunk), tab), pl.BlockSpec((1, chunk, LANES), tab),
                pl.BlockSpec((1, chunk, LANES), tab)]
    args = [q.reshape(batch, s_len, hk), k.reshape(batch, s_len, hk),
            v.reshape(batch, s_len, hv), g.reshape(batch, s_len, hv), dm, qd, kd]
    aliases = {}
    fill_layers = 0
    if state_all is None:
        st_shape = (batch, RET_HEADS, RET_DK, RET_DV)
        st_spec = pl.BlockSpec((nb, 1, RET_DK, RET_DV), lambda b, h, c: (b, h, 0, 0))
    else:
        n_layers = state_all.shape[0]
        st_shape = state_all.shape
        in_specs.append(pl.BlockSpec((None, nb, 1, RET_DK, RET_DV), lambda b, h, c: (layer, b, h, 0, 0)))
        args.append(state_all)
        if out_all is None:
            fill_layers = n_layers
            st_spec = pl.BlockSpec((n_layers, nb, 1, RET_DK, RET_DV), lambda b, h, c: (0, b, h, 0, 0))
        else:
            in_specs.append(pl.BlockSpec(memory_space=pl.ANY))
            args.append(out_all)
            aliases = {len(args) - 1: 1}
            st_spec = pl.BlockSpec((None, nb, 1, RET_DK, RET_DV), lambda b, h, c: (layer, b, h, 0, 0))
    kern = functools.partial(_ret_core_kernel, nb=nb, chunk=chunk, n_chunks=n_chunks,
                             has_state=state_all is not None, layer=layer, fill_layers=fill_layers)
    if aliases:
        inner = kern
        n_in = len(args)
        kern = lambda *refs: inner(*refs[:n_in - 1], *refs[n_in:])
    o, st = pl.pallas_call(
        kern,
        grid=(batch // nb, RET_HEADS, n_chunks),
        in_specs=in_specs,
        out_specs=[pl.BlockSpec((nb, chunk, RET_DV), seq), st_spec],
        out_shape=[jax.ShapeDtypeStruct((batch, s_len, hv), BF16),
                   jax.ShapeDtypeStruct(st_shape, F32)],
        scratch_shapes=[pltpu.VMEM((nb, RET_DK, RET_DV), F32)],
        input_output_aliases=aliases,
        compiler_params=_params("arbitrary", "arbitrary", "arbitrary"),
        name="ret_core",
    )(*args)
    return o.reshape(batch * s_len, hv), st


def _ret_prompt_kernel(q_ref, k_ref, v_ref, g_ref, dm_ref, qd_ref, kd_ref, o_ref, stout_ref, st_scr, *,
                       chunk, n_chunks):
    c = pl.program_id(1)
    dk, dv = RET_DK, RET_DV

    @pl.when(c == 0)
    def _():
        st_scr[...] = jnp.zeros((RET_HEADS, dk, dv), F32)

    for h in range(RET_HEADS):
        q = q_ref[0, :, h * dk:(h + 1) * dk]
        kf = k_ref[0, :, h * dk:(h + 1) * dk]
        v = v_ref[0, :, h * dv:(h + 1) * dv]
        st = st_scr[h]
        qd = qd_ref[h]
        kd_k = jnp.concatenate([kd_ref[h]] * (dk // LANES), axis=1)
        sc = _dot_nt(q, kf.astype(BF16))
        cross = _dot(q, st.astype(BF16))
        kv = _dot_tn((kf * kd_k).astype(BF16), v)
        inner = _dot((sc * dm_ref[h]).astype(BF16), v)
        o = inner + cross * jnp.concatenate([qd] * (dv // LANES), axis=1)
        st_scr[h] = qd[chunk - 1:chunk, 0:1] * st + kv
        on = o * lax.rsqrt(jnp.mean(o * o, axis=-1, keepdims=True) + EPS)
        gg = g_ref[0, :, h * dv:(h + 1) * dv]
        o_ref[0, :, h * dv:(h + 1) * dv] = (on * (gg * jax.nn.sigmoid(gg))).astype(o_ref.dtype)

    @pl.when(c == n_chunks - 1)
    def _():
        stout_ref[0] = st_scr[...]


def _ret_prompt(q, k, v, g, tabs, *, batch, chunk):
    hk = RET_HEADS * RET_DK
    hv = RET_HEADS * RET_DV
    s_len = q.shape[0] // batch
    n_chunks = s_len // chunk
    assert s_len % chunk == 0
    dm, qd, kd = tabs
    seq = lambda b, c: (b, c, 0)
    tab = lambda b, c: (0, 0, 0)
    o, st = pl.pallas_call(
        functools.partial(_ret_prompt_kernel, chunk=chunk, n_chunks=n_chunks),
        grid=(batch, n_chunks),
        in_specs=[pl.BlockSpec((1, chunk, hk), seq), pl.BlockSpec((1, chunk, hk), seq),
                  pl.BlockSpec((1, chunk, hv), seq), pl.BlockSpec((1, chunk, hv), seq),
                  pl.BlockSpec((RET_HEADS, chunk, chunk), tab), pl.BlockSpec((RET_HEADS, chunk, LANES), tab),
                  pl.BlockSpec((RET_HEADS, chunk, LANES), tab)],
        out_specs=[pl.BlockSpec((1, chunk, hv), seq),
                   pl.BlockSpec((1, RET_HEADS, RET_DK, RET_DV), lambda b, c: (b, 0, 0, 0))],
        out_shape=[jax.ShapeDtypeStruct((batch, s_len, hv), BF16),
                   jax.ShapeDtypeStruct((batch, RET_HEADS, RET_DK, RET_DV), F32)],
        scratch_shapes=[pltpu.VMEM((RET_HEADS, RET_DK, RET_DV), F32)],
        compiler_params=_params("arbitrary", "arbitrary"),
        name="ret_prompt",
    )(q.reshape(batch, s_len, hk), k.reshape(batch, s_len, hk), v.reshape(batch, s_len, hv),
      g.reshape(batch, s_len, hv), dm, qd, kd)
    return o.reshape(batch * s_len, hv), st


def _ret_tables(chunk):
    log_g = jnp.log(1.0 - 2.0 ** (-5.0 - jnp.arange(RET_HEADS, dtype=F32)))
    i = jnp.arange(chunk, dtype=F32)
    diff = i[:, None] - i[None, :]
    dm = jnp.where(diff >= 0, jnp.exp(log_g[:, None, None] * jnp.maximum(diff, 0.0)), 0.0)
    qd = jnp.exp(log_g[:, None] * (i[None, :] + 1.0))
    kd = jnp.exp(log_g[:, None] * (chunk - 1.0 - i[None, :]))
    rep = lambda t: jnp.broadcast_to(t[:, :, None], (RET_HEADS, chunk, LANES))
    return dm, rep(qd), rep(kd)


def _mlp_kernel(*refs, tf, has_proj, has_final):
    refs = list(refs)
    x_ref = refs.pop(0)
    a_ref, wo_ref = (refs.pop(0), refs.pop(0)) if has_proj else (None, None)
    g_ref, wu_ref, wd_ref = refs.pop(0), refs.pop(0), refs.pop(0)
    gf_ref = refs.pop(0) if has_final else None
    o_ref, = refs
    x = x_ref[...]
    if has_proj:
        x = x + _dot(a_ref[...].astype(BF16), wo_ref[...])
    h = _rms(x, g_ref[...]).astype(BF16)
    acc = None
    for f in range(D_FF // tf):
        a = jnp.maximum(_dot(h, wu_ref[:, f * tf:(f + 1) * tf]), 0.0)
        part = _dot((a * a).astype(BF16), wd_ref[f * tf:(f + 1) * tf, :])
        acc = part if acc is None else acc + part
    y = x + acc
    o_ref[...] = _rms(y, gf_ref[...]) if has_final else y


def _layer_weight(w_all, layer):
    return pl.BlockSpec((None,) + w_all.shape[1:], lambda i: (layer, 0, 0), pipeline_mode=pl.Buffered(1))


def _mlp(x, gamma, w_up_all, w_down_all, layer, *, tm, tf, a=None, w_o_all=None, w_o_layer=0, final_gamma=None):
    m, d = x.shape
    tm = _row_tile(m, tm)
    row = lambda i: (i, 0)
    const = lambda i: (0, 0)
    in_specs = [pl.BlockSpec((tm, d), row)]
    args = [x]
    if a is not None:
        in_specs += [pl.BlockSpec((tm, a.shape[1]), row), _layer_weight(w_o_all, w_o_layer)]
        args += [a, w_o_all]
    in_specs += [pl.BlockSpec((1, d), const), _layer_weight(w_up_all, layer), _layer_weight(w_down_all, layer)]
    args += [gamma, w_up_all, w_down_all]
    if final_gamma is not None:
        in_specs.append(pl.BlockSpec((1, d), const))
        args.append(final_gamma)
    return pl.pallas_call(
        functools.partial(_mlp_kernel, tf=tf, has_proj=a is not None, has_final=final_gamma is not None),
        grid=(m // tm,),
        in_specs=in_specs,
        out_specs=pl.BlockSpec((tm, d), row),
        out_shape=jax.ShapeDtypeStruct((m, d), F32),
        compiler_params=_params("arbitrary"),
        name="mlp",
    )(*args)


def _conv_kernel(*refs, tiles_per_seq, seq_in_tile):
    if seq_in_tile:
        x_ref, g_ref, win_ref, wc_ref, wout_ref, p1_ref, p2_ref, o_ref, u_ref = refs
    else:
        x_ref, g_ref, win_ref, wc_ref, wout_ref, o_ref, u_ref, carry_scr = refs
    d = D_MODEL
    x = x_ref[...]
    tm = x.shape[0]
    h = _rms(x, g_ref[...]).astype(BF16)
    bg = _dot(h, win_ref[:, 0:d])
    u = _dot(h, win_ref[:, d:2 * d]) * _dot(h, win_ref[:, 2 * d:3 * d])
    row = _iota((tm, 1), 0)
    r1 = pltpu.roll(u, 1, 0)
    r2 = pltpu.roll(u, 2, 0)
    if seq_in_tile:
        t = row % seq_in_tile
        um1 = jnp.where(t >= 1, r1, 0.0) + p1_ref[...]
        um2 = jnp.where(t >= 2, r2, 0.0) + p2_ref[...]
        u_ref[...] = u
    else:
        @pl.when((pl.program_id(0) % tiles_per_seq) == 0)
        def _():
            carry_scr[...] = jnp.zeros((SUBLANES, d), F32)

        carry = carry_scr[...]
        c1 = carry[SUBLANES - 1:SUBLANES]
        c2 = carry[SUBLANES - 2:SUBLANES - 1]
        um1 = jnp.where(row >= 1, r1, c1)
        um2 = jnp.where(row >= 2, r2, jnp.where(row == 1, c1, c2))
        tail = u[tm - SUBLANES:tm]
        carry_scr[...] = tail
        u_ref[...] = tail
    wc = wc_ref[...]
    y = wc[0:1] * um2 + wc[1:2] * um1 + wc[2:3] * u
    o_ref[...] = x + _dot((bg * y).astype(BF16), wout_ref[...])


def _conv_mixer(x, gamma, w_in, w_c, w_out, *, tm, seq_len, state=None):
    m, d = x.shape
    n_seq = m // seq_len
    wc_pad = jnp.zeros((SUBLANES, d), F32).at[:CONV_WIDTH].set(w_c)
    row = lambda i: (i, 0)
    const = lambda i: (0, 0)
    in_specs = [pl.BlockSpec((tm, d), row), pl.BlockSpec((1, d), const),
                pl.BlockSpec((d, 3 * d), const), pl.BlockSpec((SUBLANES, d), const),
                pl.BlockSpec((d, d), const)]
    args = [x, gamma, w_in, wc_pad, w_out]
    if state is not None:
        tm = _row_tile(m, tm)
        assert tm % seq_len == 0 and seq_len >= CONV_WIDTH - 1
        zeros = jnp.zeros((n_seq, seq_len, d), F32)
        p1 = zeros.at[:, 0].set(state[:, 1]).reshape(m, d)
        p2 = zeros.at[:, 0].set(state[:, 0]).at[:, 1].set(state[:, 1]).reshape(m, d)
        in_specs += [pl.BlockSpec((tm, d), row), pl.BlockSpec((tm, d), row)]
        args += [p1, p2]
        u_rows, u_blk, scratch = m, tm, []
        kern = functools.partial(_conv_kernel, tiles_per_seq=0, seq_in_tile=seq_len)
    else:
        tm = _row_tile(seq_len, tm)
        u_rows, u_blk = (m // tm) * SUBLANES, SUBLANES
        scratch = [pltpu.VMEM((SUBLANES, d), F32)]
        kern = functools.partial(_conv_kernel, tiles_per_seq=seq_len // tm, seq_in_tile=0)
    out, u = pl.pallas_call(
        kern,
        grid=(m // tm,),
        in_specs=in_specs,
        out_specs=[pl.BlockSpec((tm, d), row), pl.BlockSpec((u_blk, d), row)],
        out_shape=[jax.ShapeDtypeStruct((m, d), F32), jax.ShapeDtypeStruct((u_rows, d), F32)],
        scratch_shapes=scratch,
        compiler_params=_params("arbitrary"),
        name="conv_mixer",
    )(*args)
    u_last = u.reshape(n_seq, -1, d)[:, -(CONV_WIDTH - 1):]
    return out, u_last


def _topk_bias(gate, blk, valid, axis):
    n = gate.shape[axis]
    g = jnp.where(valid, gate, NEG_INF)
    sel = jnp.zeros(gate.shape, jnp.bool_)
    for _ in range(MOBA_TOPK):
        mx = jnp.max(g, axis=axis, keepdims=True)
        idx = jnp.min(jnp.where(g == mx, blk, n), axis=axis, keepdims=True)
        pick = blk == idx
        sel = sel | (pick & valid)
        g = jnp.where(pick, NEG_INF, g)
    return jnp.where(sel, 0.0, NEG_INF)


def _moba_prompt_kernel(qt_ref, k_ref, vt_ref, km_ref, o_ref, bias_scr, *, nblk, n_tiles):
    i = pl.program_id(2)
    bs = MOBA_BLOCK
    hd = MOBA_HD
    per_tile = LANES // hd
    heads = [(t, h2) for t in range(n_tiles) for h2 in range(per_tile)]
    feat_row = _iota((LANES, 1), 0) // hd
    blk = _iota((nblk, 1), 0)
    causal = _iota((bs, 1), 0) <= _iota((1, bs), 1)
    valid = blk < i

    def key_block(j):
        start = pl.multiple_of(j * bs, bs)
        return k_ref[pl.ds(start, bs), :], vt_ref[0, :, pl.ds(start, bs)]

    def scores(kj, qbs):
        return [_dot(kj[:, t * LANES:(t + 1) * LANES], qb) for (t, _), qb in zip(heads, qbs)]

    def values(vtj, ps):
        return [_dot(vtj[n * hd:(n + 1) * hd], p.astype(BF16)) for n, p in enumerate(ps)]

    qbs = []
    for n, (t, h2) in enumerate(heads):
        qt = qt_ref[0, t * LANES:(t + 1) * LANES, :]
        km = km_ref[:, t * LANES:(t + 1) * LANES]
        qb = jnp.where(feat_row == h2, qt * (hd ** -0.5 * LOG2_E), 0.0).astype(BF16)
        qbs.append(qb)
        bias_scr[n] = _topk_bias(_dot(km.astype(BF16), qb), blk, valid, 0)

    k_own, vt_own = key_block(i)
    ss = [jnp.where(causal, s, NEG_INF) for s in scores(k_own, qbs)]
    ms = [jnp.max(s, axis=0, keepdims=True) for s in ss]
    ps = [jnp.exp2(s - m) for s, m in zip(ss, ms)]
    ls = [jnp.sum(p, axis=0, keepdims=True) for p in ps]
    accs = values(vt_own, ps)

    def body(jj, carry):
        ms, ls, accs = carry
        blocks = [key_block(2 * jj + u) for u in range(2)]
        ss = [scores(blocks[u][0], qbs) for u in range(2)]
        biases = [[bias_scr[n, pl.ds(2 * jj + u, 1), :] for n in range(len(heads))] for u in range(2)]
        tops = [[jnp.max(s, axis=0, keepdims=True) + bias for s, bias in zip(ss[u], biases[u])] for u in range(2)]
        m_new = [jnp.maximum(m, jnp.maximum(t0, t1)) for m, t0, t1 in zip(ms, tops[0], tops[1])]
        alphas = [jnp.exp2(m - mn) for m, mn in zip(ms, m_new)]
        ps = [[jnp.exp2(s - (mn - bias)) for s, mn, bias in zip(ss[u], m_new, biases[u])] for u in range(2)]
        ls = [a * l + jnp.sum(p0, axis=0, keepdims=True) + jnp.sum(p1, axis=0, keepdims=True)
              for a, l, p0, p1 in zip(alphas, ls, ps[0], ps[1])]
        pvs = [values(blocks[u][1], ps[u]) for u in range(2)]
        accs = [a * acc + pv0 + pv1 for a, acc, pv0, pv1 in zip(alphas, accs, pvs[0], pvs[1])]
        return m_new, ls, accs

    ms, ls, accs = lax.fori_loop(0, (i + 1) // 2, body, (ms, ls, accs))
    ot = jnp.concatenate([acc / l for acc, l in zip(accs, ls)], axis=0)
    o_ref[...] = ot.T.astype(o_ref.dtype)


def _moba_prompt(qt, k, vt, kmean):
    batch, d, s_len = qt.shape
    nblk = s_len // MOBA_BLOCK
    assert s_len % MOBA_BLOCK == 0 and nblk % SUBLANES == 0
    n_tiles = MOBA_TILES_PER_STEP
    width = n_tiles * LANES
    km2 = kmean.reshape(batch * nblk, d)
    return pl.pallas_call(
        functools.partial(_moba_prompt_kernel, nblk=nblk, n_tiles=n_tiles),
        grid=(batch, d // width, nblk),
        in_specs=[pl.BlockSpec((1, width, MOBA_BLOCK), lambda b, h, i: (b, h, i)),
                  pl.BlockSpec((s_len, width), lambda b, h, i: (b, h)),
                  pl.BlockSpec((1, width, s_len), lambda b, h, i: (b, h, 0)),
                  pl.BlockSpec((nblk, width), lambda b, h, i: (b, h))],
        out_specs=pl.BlockSpec((MOBA_BLOCK, width), lambda b, h, i: (b * nblk + i, h)),
        out_shape=jax.ShapeDtypeStruct((batch * s_len, d), BF16),
        scratch_shapes=[pltpu.VMEM((width // MOBA_HD, nblk, MOBA_BLOCK), F32)],
        compiler_params=_params("arbitrary", "arbitrary", "arbitrary"),
        name="moba_prompt",
    )(qt, k, vt, km2)


def _moba_paged_kernel(pt_ref, q_ref, kn_ref, vn_ref, *refs, n_pages, page, t_new):
    k_refs = refs[0:n_pages]
    v_refs = refs[n_pages:2 * n_pages]
    o_ref = refs[2 * n_pages]
    qbd_scr, s_scr, bias_scr, m_scr = refs[2 * n_pages + 1:]
    del pt_ref
    ph = pl.program_id(1)
    ppb = MOBA_BLOCK // page
    n_blk = n_pages // ppb
    n_col = MOBA_HEADS * t_new
    d = D_MODEL
    blk_lane = _iota((1, LANES), 1)
    own_ok = _iota((n_col, t_new), 1) <= _iota((n_col, t_new), 0) % t_new

    def own_scores(qbd):
        return jnp.where(own_ok, _dot_nt(qbd, kn_ref[0].astype(BF16)), NEG_INF)

    def block_bias(bias, p):
        b = p // ppb
        return bias[:, b:b + 1]

    @pl.when(ph == 0)
    def _():
        q = q_ref[0]
        qt = jnp.concatenate([q] * MOBA_HEADS, axis=0)
        bd = _iota((n_col, d), 0) // t_new == _iota((n_col, d), 1) // MOBA_HD
        qbd = jnp.where(bd, qt * (MOBA_HD ** -0.5), 0.0).astype(BF16)
        qbd_scr[...] = qbd
        gate = jnp.zeros((n_col, LANES), F32)
        for b in range(n_blk):
            ssum = None
            for r in range(ppb):
                sc = _dot(qbd, k_refs[b * ppb + r][0].astype(BF16))
                s_scr[b * ppb + r] = sc
                ssum = sc if ssum is None else ssum + sc
            gate = jnp.where(blk_lane == b, jnp.sum(ssum, axis=1, keepdims=True), gate)
        bias = _topk_bias(gate, blk_lane, blk_lane < n_blk, 1)
        bias_scr[...] = bias
        top = None
        for p in range(n_pages):
            sb = s_scr[p] + block_bias(bias, p)
            top = sb if top is None else jnp.maximum(top, sb)
        m = jnp.maximum(jnp.max(own_scores(qbd), axis=1, keepdims=True), jnp.max(top, axis=1, keepdims=True))
        m_scr[...] = jnp.broadcast_to(m, (n_col, LANES))

    @pl.when(ph == 1)
    def _():
        bias = bias_scr[...]
        m = m_scr[:, 0:1]
        e_own = jnp.exp(own_scores(qbd_scr[...]) - m)
        esum = None
        for p in range(n_pages):
            e = jnp.exp(s_scr[p] + block_bias(bias, p) - m)
            s_scr[p] = e
            esum = e if esum is None else esum + e
        inv = 1.0 / (jnp.sum(e_own, axis=1, keepdims=True) + jnp.sum(esum, axis=1, keepdims=True))
        acc = _dot((e_own * inv).astype(BF16), vn_ref[0].astype(BF16))
        for p in range(n_pages):
            acc = acc + _dot_nt((s_scr[p] * inv).astype(BF16), v_refs[p][0].astype(BF16))
        bd = _iota((n_col, d), 0) // t_new == _iota((n_col, d), 1) // MOBA_HD
        a = jnp.where(bd, acc, 0.0)
        out = a[0:t_new]
        for hd in range(1, MOBA_HEADS):
            out = out + a[hd * t_new:(hd + 1) * t_new]
        o_ref[0] = out


def _moba_paged(q, k_new, v_new, cache_kt, cache_vt, page_table):
    db, t_new, d = q.shape
    n_pages = page_table.shape[1]
    page = cache_kt.shape[2]
    past = n_pages * page
    assert past % MOBA_BLOCK == 0 and MOBA_BLOCK % page == 0
    assert t_new <= MOBA_BLOCK and MOBA_TOPK <= past // MOBA_BLOCK <= LANES
    n_col = MOBA_HEADS * t_new
    pt = page_table.reshape(-1).astype(jnp.int32)

    def k_map(r):
        return lambda b, ph, pt_ref: (pt_ref[b * n_pages + r], 0, 0)

    def v_map(r):
        return lambda b, ph, pt_ref: (pt_ref[jnp.where(ph == 0, jnp.maximum(b - 1, 0), b) * n_pages + r], 0, 0)

    tok = pl.BlockSpec((1, t_new, d), lambda b, ph, pt_ref: (b, 0, 0))
    in_specs = [tok, tok, tok]
    in_specs += [pl.BlockSpec((1, d, page), k_map(r)) for r in range(n_pages)]
    in_specs += [pl.BlockSpec((1, d, page), v_map(r)) for r in range(n_pages)]
    grid_spec = pltpu.PrefetchScalarGridSpec(
        num_scalar_prefetch=1,
        grid=(db, 2),
        in_specs=in_specs,
        out_specs=pl.BlockSpec((1, t_new, d), lambda b, ph, pt_ref: (b, 0, 0)),
        scratch_shapes=[pltpu.VMEM((n_col, d), BF16), pltpu.VMEM((n_pages, n_col, page), F32),
                        pltpu.VMEM((n_col, LANES), F32), pltpu.VMEM((n_col, LANES), F32)],
    )
    return pl.pallas_call(
        functools.partial(_moba_paged_kernel, n_pages=n_pages, page=page, t_new=t_new),
        grid_spec=grid_spec,
        out_shape=jax.ShapeDtypeStruct((db, t_new, d), F32),
        compiler_params=_params("arbitrary", "arbitrary"),
        name="moba_paged",
    )(pt, q, k_new, v_new, *([cache_kt] * n_pages), *([cache_vt] * n_pages))


def _feature_major_pages(cache):
    n_phys, page, n_h, hd = cache.shape
    return jnp.transpose(cache, (0, 2, 3, 1)).reshape(n_phys, n_h * hd, page)


def _token_rows(xt, batch):
    _, _, s_len = xt.shape
    return jnp.transpose(xt.reshape(batch, MOBA_HEADS, MOBA_HD, s_len), (0, 3, 1, 2))


def kernel(x_prompt, x_sample, state_ret, cache_k, cache_v, state_conv, page_table, norm_mix, norm_ff, norm_final, ret_w_qkvg, ret_w_o, moba_w_qkv, moba_w_o, conv_w_in, conv_w, conv_w_out, mlp_w_up, mlp_w_down):
    b, s, d = x_prompt.shape
    db, t, _ = x_sample.shape
    depth = norm_mix.shape[0]
    n_pages = page_table.shape[1]
    page = cache_k.shape[2]
    past_len = n_pages * page

    xp = x_prompt.reshape(b * s, d)
    xs = x_sample.reshape(db * t, d)
    tm_p = TM_PROMPT
    tm_s = db * t

    cos_p, sin_p = _rope_tables(jnp.arange(s, dtype=jnp.int32))
    cos_s, sin_s = _rope_tables(past_len + jnp.arange(t, dtype=jnp.int32))
    cos_s = jnp.tile(cos_s, (tm_s // t, 1))
    sin_s = jnp.tile(sin_s, (tm_s // t, 1))
    tabs_p = _ret_tables(RET_CHUNK)
    tabs_s = _ret_tables(t)

    ret_w = ret_w_qkvg.astype(BF16)
    ret_wo = ret_w_o.astype(BF16)
    moba_wo = moba_w_o.astype(BF16)
    up_w = mlp_w_up.astype(BF16)
    down_w = mlp_w_down.astype(BF16)

    ret_p, ret_s_all, kp_l, vp_l, ks_l, vs_l, conv_p, conv_s = [], None, [], [], [], [], [], []
    for i in range(depth):
        kind = i % N_MIXERS
        j = i // N_MIXERS
        g_mix = norm_mix[i].reshape(1, d)
        g_ff = norm_ff[i].reshape(1, d)
        if kind == 0:
            wo_all = ret_wo
            q, k, v, g = _ret_proj(xp, g_mix, ret_w, j, cos_p, sin_p, tm=tm_p, qv_dtype=BF16)
            op, st_p = _ret_prompt(q, k, v, g, tabs_p, batch=b, chunk=RET_CHUNK)
            q, k, v, g = _ret_proj(xs, g_mix, ret_w, j, cos_s, sin_s, tm=tm_s, qv_dtype=F32)
            os_, ret_s_all = _ret_core(q, k, v, g, tabs_s, batch=db, chunk=t, nb=RET_SEQS_PER_STEP,
                                       state_all=state_ret, layer=j, out_all=ret_s_all)
            ret_p.append(st_p)
        elif kind == 1:
            wo_all = moba_wo
            qt, k, kt, vt, vtb, kmean = _moba_proj_t(xp, g_mix, moba_w_qkv[j], batch=b, tm=tm_p)
            op = _moba_prompt(qt, k, vtb, kmean)
            kp_l.append(_token_rows(kt, b))
            vp_l.append(_token_rows(vt, b))
            q, k, v = _norm_proj(xs, g_mix, moba_w_qkv[j].astype(BF16), (d, d, d), (F32, F32, F32), tm=tm_s)
            os_ = _moba_paged(q.reshape(db, t, d), k.reshape(db, t, d), v.reshape(db, t, d),
                              _feature_major_pages(cache_k[j]), _feature_major_pages(cache_v[j]),
                              page_table).reshape(db * t, d)
            ks_l.append(k.reshape(db, t, MOBA_HEADS, MOBA_HD))
            vs_l.append(v.reshape(db, t, MOBA_HEADS, MOBA_HD))
        else:
            w_in = conv_w_in[j].astype(BF16)
            w_out = conv_w_out[j].astype(BF16)
            xp, buf_p = _conv_mixer(xp, g_mix, w_in, conv_w[j], w_out, tm=tm_p, seq_len=s)
            xs, buf_s = _conv_mixer(xs, g_mix, w_in, conv_w[j], w_out, tm=tm_s, seq_len=t,
                                    state=state_conv[j])
            conv_p.append(buf_p)
            conv_s.append(buf_s)
            op = os_ = wo_all = None
        g_fin = norm_final.reshape(1, d) if i == depth - 1 else None
        xp = _mlp(xp, g_ff, up_w, down_w, i, tm=TM_MLP, tf=TF_MLP, a=op, w_o_all=wo_all, w_o_layer=j,
                  final_gamma=g_fin)
        xs = _mlp(xs, g_ff, up_w, down_w, i, tm=tm_s, tf=TF_MLP, a=os_, w_o_all=wo_all, w_o_layer=j,
                  final_gamma=g_fin)
    y_prompt = xp.reshape(b, s, d)
    y_sample = xs.reshape(db, t, d)
    return (y_prompt, y_sample, jnp.stack(ret_p), ret_s_all, jnp.stack(kp_l), jnp.stack(vp_l),
            jnp.stack(ks_l), jnp.stack(vs_l), jnp.stack(conv_p), jnp.stack(conv_s))
```

```python
import functools

import jax
import jax.numpy as jnp
from jax import lax
from jax.experimental import pallas as pl
from jax.experimental.pallas import tpu as pltpu

F32 = jnp.float32
BF16 = jnp.bfloat16
NEG_INF = float("-inf")

D_MODEL = 1024
N_MIXERS = 3
RET_HEADS = 4
RET_DK = D_MODEL // RET_HEADS
RET_DV = 2 * RET_DK
ROPE_BASE = 10000.0
MOBA_HEADS = 16
MOBA_HD = D_MODEL // MOBA_HEADS
MOBA_BLOCK = 256
MOBA_TOPK = 3
CONV_WIDTH = 3
D_FF = 4 * D_MODEL
EPS = 1e-6
LOG2_E = 1.4426950408889634

LANES = 128
SUBLANES = 8

TM_PROMPT = 512
TM_MLP = 512
TF_MLP = 1024
RET_CHUNK = 256
RET_SEQS_PER_STEP = 8
MOBA_TILES_PER_STEP = 8


def _params(*sem):
    return pltpu.CompilerParams(dimension_semantics=sem)


def _row_tile(m, want):
    t = min(m, want)
    assert m % t == 0, (m, t)
    return t


def _rms(x, gamma):
    ms = jnp.mean(x * x, axis=-1, keepdims=True)
    return x * lax.rsqrt(ms + EPS) * gamma


def _dot(a, b):
    return jnp.dot(a, b, preferred_element_type=F32)


def _dot_nt(a, b):
    return lax.dot_general(a, b, (((1,), (1,)), ((), ())), preferred_element_type=F32)


def _dot_tn(a, b):
    return lax.dot_general(a, b, (((0,), (0,)), ((), ())), preferred_element_type=F32)


def _iota(shape, dim):
    return lax.broadcasted_iota(jnp.int32, shape, dim)


def _norm_proj_kernel(x_ref, g_ref, w_ref, *out_refs, widths):
    h = _rms(x_ref[...], g_ref[...]).astype(BF16)
    off = 0
    for idx, wd in enumerate(widths):
        out_refs[idx][...] = _dot(h, w_ref[:, off:off + wd]).astype(out_refs[idx].dtype)
        off += wd


def _norm_proj(x, gamma, w, widths, dtypes, *, tm):
    m, d = x.shape
    tm = _row_tile(m, tm)
    n = sum(widths)
    return pl.pallas_call(
        functools.partial(_norm_proj_kernel, widths=tuple(widths)),
        grid=(m // tm,),
        in_specs=[pl.BlockSpec((tm, d), lambda i: (i, 0)),
                  pl.BlockSpec((1, d), lambda i: (0, 0)),
                  pl.BlockSpec((d, n), lambda i: (0, 0))],
        out_specs=[pl.BlockSpec((tm, wd), lambda i: (i, 0)) for wd in widths],
        out_shape=[jax.ShapeDtypeStruct((m, wd), dt) for wd, dt in zip(widths, dtypes)],
        compiler_params=_params("arbitrary"),
        name="norm_proj",
    )(x, gamma, w)


def _moba_proj_t_kernel(x_ref, g_ref, wt_ref, qt_ref, k_ref, kt_ref, vt_ref, vtb_ref, km_ref):
    d = D_MODEL
    h = _rms(x_ref[...], g_ref[...]).astype(BF16)
    qt_ref[0] = _dot_nt(wt_ref[0:d, :], h)
    kt = _dot_nt(wt_ref[d:2 * d, :], h)
    kt_ref[0] = kt
    vt = _dot_nt(wt_ref[2 * d:3 * d, :], h)
    vt_ref[0] = vt
    vtb_ref[0] = vt.astype(BF16)
    k = kt.T
    k_ref[...] = k.astype(BF16)
    for r in range(k.shape[0] // MOBA_BLOCK):
        km_ref[r] = jnp.mean(k[r * MOBA_BLOCK:(r + 1) * MOBA_BLOCK], axis=0, keepdims=True)


def _moba_proj_t(x, gamma, w_qkv, *, batch, tm):
    m, d = x.shape
    s_len = m // batch
    tm = _row_tile(s_len, tm)
    assert tm % MOBA_BLOCK == 0
    per_seq = s_len // tm
    w_t = w_qkv.T.astype(BF16)
    feat = lambda i: (i // per_seq, 0, i % per_seq)
    t_shape = jax.ShapeDtypeStruct((batch, d, s_len), F32)
    return pl.pallas_call(
        _moba_proj_t_kernel,
        grid=(m // tm,),
        in_specs=[pl.BlockSpec((tm, d), lambda i: (i, 0)),
                  pl.BlockSpec((1, d), lambda i: (0, 0)),
                  pl.BlockSpec((3 * d, d), lambda i: (0, 0))],
        out_specs=[pl.BlockSpec((1, d, tm), feat),
                   pl.BlockSpec((tm, d), lambda i: (i, 0)),
                   pl.BlockSpec((1, d, tm), feat),
                   pl.BlockSpec((1, d, tm), feat),
                   pl.BlockSpec((1, d, tm), feat),
                   pl.BlockSpec((tm // MOBA_BLOCK, 1, d), lambda i: (i, 0, 0))],
        out_shape=[t_shape, jax.ShapeDtypeStruct((m, d), BF16), t_shape, t_shape,
                   jax.ShapeDtypeStruct((batch, d, s_len), BF16),
                   jax.ShapeDtypeStruct((m // MOBA_BLOCK, 1, d), F32)],
        compiler_params=_params("arbitrary"),
        name="moba_proj_t",
    )(x, gamma, w_t)


def _ret_proj_kernel(x_ref, g_ref, w_ref, cos_ref, sin_ref, q_ref, k_ref, v_ref, gate_ref):
    h = _rms(x_ref[...], g_ref[...]).astype(BF16)
    hk = RET_HEADS * RET_DK
    hv = RET_HEADS * RET_DV
    even = (_iota((1, LANES), 1) % 2) == 0
    for off, scale, out in ((0, 1.0, q_ref), (hk, RET_DK ** -0.5, k_ref)):
        p = _dot(h, w_ref[:, off:off + hk])
        for c in range(hk // LANES):
            x = p[:, c * LANES:(c + 1) * LANES]
            swapped = jnp.where(even, pltpu.roll(x, LANES - 1, 1), pltpu.roll(x, 1, 1))
            tab = pl.ds((c * LANES) % RET_DK, LANES)
            rot = x * cos_ref[:, tab] + swapped * sin_ref[:, tab]
            out[:, c * LANES:(c + 1) * LANES] = (rot * scale).astype(out.dtype)
    v_ref[...] = _dot(h, w_ref[:, 2 * hk:2 * hk + hv]).astype(v_ref.dtype)
    gate_ref[...] = _dot(h, w_ref[:, 2 * hk + hv:])


def _ret_proj(x, gamma, w_all, layer, cos, sin, *, tm, qv_dtype):
    m, d = x.shape
    tm = _row_tile(m, tm)
    hk = RET_HEADS * RET_DK
    hv = RET_HEADS * RET_DV
    n_tab = cos.shape[0] // tm
    assert cos.shape[0] % tm == 0
    row = lambda i: (i, 0)
    return pl.pallas_call(
        _ret_proj_kernel,
        grid=(m // tm,),
        in_specs=[pl.BlockSpec((tm, d), row),
                  pl.BlockSpec((1, d), lambda i: (0, 0)),
                  pl.BlockSpec((None, d, 2 * hk + 2 * hv), lambda i: (layer, 0, 0)),
                  pl.BlockSpec((tm, RET_DK), lambda i: (i % n_tab, 0)),
                  pl.BlockSpec((tm, RET_DK), lambda i: (i % n_tab, 0))],
        out_specs=[pl.BlockSpec((tm, hk), row), pl.BlockSpec((tm, hk), row),
                   pl.BlockSpec((tm, hv), row), pl.BlockSpec((tm, hv), row)],
        out_shape=[jax.ShapeDtypeStruct((m, hk), qv_dtype), jax.ShapeDtypeStruct((m, hk), F32),
                   jax.ShapeDtypeStruct((m, hv), qv_dtype), jax.ShapeDtypeStruct((m, hv), F32)],
        compiler_params=_params("arbitrary"),
        name="ret_proj",
    )(x, gamma, w_all, cos, sin)


def _rope_tables(pos):
    inv = ROPE_BASE ** (-jnp.arange(0, RET_DK, 2, dtype=F32) / RET_DK)
    ang = pos.astype(F32)[:, None] * inv[None, :]
    cos = jnp.repeat(jnp.cos(ang), 2, axis=1)
    sin = jnp.sin(ang)
    return cos, jnp.stack([-sin, sin], axis=-1).reshape(cos.shape)


def _ret_core_kernel(*refs, nb, chunk, n_chunks, has_state, layer, fill_layers):
    refs = list(refs)
    q_ref, k_ref, v_ref, g_ref, dm_ref, qd_ref, kd_ref = refs[:7]
    st0_ref = refs[7] if has_state else None
    o_ref, stout_ref, st_scr = refs[-3:]
    c = pl.program_id(2)

    @pl.when(c == 0)
    def _():
        for s in range(nb):
            st_scr[s] = st0_ref[s, 0] if has_state else jnp.zeros((RET_DK, RET_DV), F32)

    dm = dm_ref[0]
    qd = qd_ref[0]
    kd = kd_ref[0]
    qd_v = jnp.concatenate([qd] * (RET_DV // LANES), axis=1)
    kd_k = jnp.concatenate([kd] * (RET_DK // LANES), axis=1)
    sdec = qd[chunk - 1:chunk, 0:1]
    for s in range(nb):
        q = q_ref[s].astype(BF16)
        kf = k_ref[s]
        v = v_ref[s].astype(BF16)
        st = st_scr[s]
        sc = _dot_nt(q, kf.astype(BF16)) * dm
        inner = _dot(sc.astype(BF16), v)
        cross = _dot(q, st.astype(BF16)) * qd_v
        o = inner + cross
        kv = _dot_tn((kf * kd_k).astype(BF16), v)
        st_new = sdec * st + kv
        st_scr[s] = st_new
        on = o * lax.rsqrt(jnp.mean(o * o, axis=-1, keepdims=True) + EPS)
        gg = g_ref[s]
        o_ref[s] = (on * (gg * jax.nn.sigmoid(gg))).astype(o_ref.dtype)

        @pl.when(c == n_chunks - 1)
        def _():
            if fill_layers:
                for jj in range(fill_layers):
                    stout_ref[jj, s, 0] = st_new if jj == layer else jnp.zeros_like(st_new)
            else:
                stout_ref[s, 0] = st_new


def _ret_core(q, k, v, g, tabs, *, batch, chunk, nb, state_all=None, layer=0, out_all=None):
    hk = RET_HEADS * RET_DK
    hv = RET_HEADS * RET_DV
    s_len = q.shape[0] // batch
    n_chunks = s_len // chunk
    assert s_len % chunk == 0 and batch % nb == 0
    dm, qd, kd = tabs
    seq = lambda b, h, c: (b, c, h)
    tab = lambda b, h, c: (h, 0, 0)
    in_specs = [pl.BlockSpec((nb, chunk, RET_DK), seq), pl.BlockSpec((nb, chunk, RET_DK), seq),
                pl.BlockSpec((nb, chunk, RET_DV), seq), pl.BlockSpec((nb, chunk, RET_DV), seq),
                pl.BlockSpec((1, chunk, chunk), tab), pl.BlockSpec((1, chunk, LANES), tab),
                pl.BlockSpec((1, chunk, LANES), tab)]
    args = [q.reshape(batch, s_len, hk), k.reshape(batch, s_len, hk),
            v.reshape(batch, s_len, hv), g.reshape(batch, s_len, hv), dm, qd, kd]
    aliases = {}
    fill_layers = 0
    if state_all is None:
        st_shape = (batch, RET_HEADS, RET_DK, RET_DV)
        st_spec = pl.BlockSpec((nb, 1, RET_DK, RET_DV), lambda b, h, c: (b, h, 0, 0))
    else:
        n_layers = state_all.shape[0]
        st_shape = state_all.shape
        in_specs.append(pl.BlockSpec((None, nb, 1, RET_DK, RET_DV), lambda b, h, c: (layer, b, h, 0, 0)))
        args.append(state_all)
        if out_all is None:
            fill_layers = n_layers
            st_spec = pl.BlockSpec((n_layers, nb, 1, RET_DK, RET_DV), lambda b, h, c: (0, b, h, 0, 0))
        else:
            in_specs.append(pl.BlockSpec(memory_space=pl.ANY))
            args.append(out_all)
            aliases = {len(args) - 1: 1}
            st_spec = pl.BlockSpec((None, nb, 1, RET_DK, RET_DV), lambda b, h, c: (layer, b, h, 0, 0))
    kern = functools.partial(_ret_core_kernel, nb=nb, chunk=chunk, n_chunks=n_chunks,
                             has_state=state_all is not None, layer=layer, fill_layers=fill_layers)
    if aliases:
        inner = kern
        n_in = len(args)
        kern = lambda *refs: inner(*refs[:n_in - 1], *refs[n_in:])
    o, st = pl.pallas_call(
        kern,
        grid=(batch // nb, RET_HEADS, n_chunks),
        in_specs=in_specs,
        out_specs=[pl.BlockSpec((nb, chunk, RET_DV), seq), st_spec],
        out_shape=[jax.ShapeDtypeStruct((batch, s_len, hv), BF16),
                   jax.ShapeDtypeStruct(st_shape, F32)],
        scratch_shapes=[pltpu.VMEM((nb, RET_DK, RET_DV), F32)],
        input_output_aliases=aliases,
        compiler_params=_params("arbitrary", "arbitrary", "arbitrary"),
        name="ret_core",
    )(*args)
    return o.reshape(batch * s_len, hv), st


def _ret_prompt_kernel(q_ref, k_ref, v_ref, g_ref, dm_ref, qd_ref, kd_ref, o_ref, stout_ref, st_scr, *,
                       chunk, n_chunks):
    c = pl.program_id(1)
    dk, dv = RET_DK, RET_DV

    @pl.when(c == 0)
    def _():
        st_scr[...] = jnp.zeros((RET_HEADS, dk, dv), F32)

    for h in range(RET_HEADS):
        q = q_ref[0, :, h * dk:(h + 1) * dk]
        kf = k_ref[0, :, h * dk:(h + 1) * dk]
        v = v_ref[0, :, h * dv:(h + 1) * dv]
        st = st_scr[h]
        qd = qd_ref[h]
        kd_k = jnp.concatenate([kd_ref[h]] * (dk // LANES), axis=1)
        sc = _dot_nt(q, kf.astype(BF16))
        cross = _dot(q, st.astype(BF16))
        kv = _dot_tn((kf * kd_k).astype(BF16), v)
        inner = _dot((sc * dm_ref[h]).astype(BF16), v)
        o = inner + cross * jnp.concatenate([qd] * (dv // LANES), axis=1)
        st_scr[h] = qd[chunk - 1:chunk, 0:1] * st + kv
        on = o * lax.rsqrt(jnp.mean(o * o, axis=-1, keepdims=True) + EPS)
        gg = g_ref[0, :, h * dv:(h + 1) * dv]
        o_ref[0, :, h * dv:(h + 1) * dv] = (on * (gg * jax.nn.sigmoid(gg))).astype(o_ref.dtype)

    @pl.when(c == n_chunks - 1)
    def _():
        stout_ref[0] = st_scr[...]


def _ret_prompt(q, k, v, g, tabs, *, batch, chunk):
    hk = RET_HEADS * RET_DK
    hv = RET_HEADS * RET_DV
    s_len = q.shape[0] // batch
    n_chunks = s_len // chunk
    assert s_len % chunk == 0
    dm, qd, kd = tabs
    seq = lambda b, c: (b, c, 0)
    tab = lambda b, c: (0, 0, 0)
    o, st = pl.pallas_call(
        functools.partial(_ret_prompt_kernel, chunk=chunk, n_chunks=n_chunks),
        grid=(batch, n_chunks),
        in_specs=[pl.BlockSpec((1, chunk, hk), seq), pl.BlockSpec((1, chunk, hk), seq),
                  pl.BlockSpec((1, chunk, hv), seq), pl.BlockSpec((1, chunk, hv), seq),
                  pl.BlockSpec((RET_HEADS, chunk, chunk), tab), pl.BlockSpec((RET_HEADS, chunk, LANES), tab),
                  pl.BlockSpec((RET_HEADS, chunk, LANES), tab)],
        out_specs=[pl.BlockSpec((1, chunk, hv), seq),
                   pl.BlockSpec((1, RET_HEADS, RET_DK, RET_DV), lambda b, c: (b, 0, 0, 0))],
        out_shape=[jax.ShapeDtypeStruct((batch, s_len, hv), BF16),
                   jax.ShapeDtypeStruct((batch, RET_HEADS, RET_DK, RET_DV), F32)],
        scratch_shapes=[pltpu.VMEM((RET_HEADS, RET_DK, RET_DV), F32)],
        compiler_params=_params("arbitrary", "arbitrary"),
        name="ret_prompt",
    )(q.reshape(batch, s_len, hk), k.reshape(batch, s_len, hk), v.reshape(batch, s_len, hv),
      g.reshape(batch, s_len, hv), dm, qd, kd)
    return o.reshape(batch * s_len, hv), st


def _ret_tables(chunk):
    log_g = jnp.log(1.0 - 2.0 ** (-5.0 - jnp.arange(RET_HEADS, dtype=F32)))
    i = jnp.arange(chunk, dtype=F32)
    diff = i[:, None] - i[None, :]
    dm = jnp.where(diff >= 0, jnp.exp(log_g[:, None, None] * jnp.maximum(diff, 0.0)), 0.0)
    qd = jnp.exp(log_g[:, None] * (i[None, :] + 1.0))
    kd = jnp.exp(log_g[:, None] * (chunk - 1.0 - i[None, :]))
    rep = lambda t: jnp.broadcast_to(t[:, :, None], (RET_HEADS, chunk, LANES))
    return dm, rep(qd), rep(kd)


def _mlp_kernel(*refs, tf, has_proj, has_final):
    refs = list(refs)
    x_ref = refs.pop(0)
    a_ref, wo_ref = (refs.pop(0), refs.pop(0)) if has_proj else (None, None)
    g_ref, wu_ref, wd_ref = refs.pop(0), refs.pop(0), refs.pop(0)
    gf_ref = refs.pop(0) if has_final else None
    o_ref, = refs
    x = x_ref[...]
    if has_proj:
        x = x + _dot(a_ref[...].astype(BF16), wo_ref[...])
    h = _rms(x, g_ref[...]).astype(BF16)
    acc = None
    for f in range(D_FF // tf):
        a = jnp.maximum(_dot(h, wu_ref[:, f * tf:(f + 1) * tf]), 0.0)
        part = _dot((a * a).astype(BF16), wd_ref[f * tf:(f + 1) * tf, :])
        acc = part if acc is None else acc + part
    y = x + acc
    o_ref[...] = _rms(y, gf_ref[...]) if has_final else y


def _layer_weight(w_all, layer):
    return pl.BlockSpec((None,) + w_all.shape[1:], lambda i: (layer, 0, 0), pipeline_mode=pl.Buffered(1))


def _mlp(x, gamma, w_up_all, w_down_all, layer, *, tm, tf, a=None, w_o_all=None, w_o_layer=0, final_gamma=None):
    m, d = x.shape
    tm = _row_tile(m, tm)
    row = lambda i: (i, 0)
    const = lambda i: (0, 0)
    in_specs = [pl.BlockSpec((tm, d), row)]
    args = [x]
    if a is not None:
        in_specs += [pl.BlockSpec((tm, a.shape[1]), row), _layer_weight(w_o_all, w_o_layer)]
        args += [a, w_o_all]
    in_specs += [pl.BlockSpec((1, d), const), _layer_weight(w_up_all, layer), _layer_weight(w_down_all, layer)]
    args += [gamma, w_up_all, w_down_all]
    if final_gamma is not None:
        in_specs.append(pl.BlockSpec((1, d), const))
        args.append(final_gamma)
    return pl.pallas_call(
        functools.partial(_mlp_kernel, tf=tf, has_proj=a is not None, has_final=final_gamma is not None),
        grid=(m // tm,),
        in_specs=in_specs,
        out_specs=pl.BlockSpec((tm, d), row),
        out_shape=jax.ShapeDtypeStruct((m, d), F32),
        compiler_params=_params("arbitrary"),
        name="mlp",
    )(*args)


def _conv_kernel(*refs, tiles_per_seq, seq_in_tile):
    if seq_in_tile:
        x_ref, g_ref, win_ref, wc_ref, wout_ref, p1_ref, p2_ref, o_ref, u_ref = refs
    else:
        x_ref, g_ref, win_ref, wc_ref, wout_ref, o_ref, u_ref, carry_scr = refs
    d = D_MODEL
    x = x_ref[...]
    tm = x.shape[0]
    h = _rms(x, g_ref[...]).astype(BF16)
    bg = _dot(h, win_ref[:, 0:d])
    u = _dot(h, win_ref[:, d:2 * d]) * _dot(h, win_ref[:, 2 * d:3 * d])
    row = _iota((tm, 1), 0)
    r1 = pltpu.roll(u, 1, 0)
    r2 = pltpu.roll(u, 2, 0)
    if seq_in_tile:
        t = row % seq_in_tile
        um1 = jnp.where(t >= 1, r1, 0.0) + p1_ref[...]
        um2 = jnp.where(t >= 2, r2, 0.0) + p2_ref[...]
        u_ref[...] = u
    else:
        @pl.when((pl.program_id(0) % tiles_per_seq) == 0)
        def _():
            carry_scr[...] = jnp.zeros((SUBLANES, d), F32)

        carry = carry_scr[...]
        c1 = carry[SUBLANES - 1:SUBLANES]
        c2 = carry[SUBLANES - 2:SUBLANES - 1]
        um1 = jnp.where(row >= 1, r1, c1)
        um2 = jnp.where(row >= 2, r2, jnp.where(row == 1, c1, c2))
        tail = u[tm - SUBLANES:tm]
        carry_scr[...] = tail
        u_ref[...] = tail
    wc = wc_ref[...]
    y = wc[0:1] * um2 + wc[1:2] * um1 + wc[2:3] * u
    o_ref[...] = x + _dot((bg * y).astype(BF16), wout_ref[...])


def _conv_mixer(x, gamma, w_in, w_c, w_out, *, tm, seq_len, state=None):
    m, d = x.shape
    n_seq = m // seq_len
    wc_pad = jnp.zeros((SUBLANES, d), F32).at[:CONV_WIDTH].set(w_c)
    row = lambda i: (i, 0)
    const = lambda i: (0, 0)
    in_specs = [pl.BlockSpec((tm, d), row), pl.BlockSpec((1, d), const),
                pl.BlockSpec((d, 3 * d), const), pl.BlockSpec((SUBLANES, d), const),
                pl.BlockSpec((d, d), const)]
    args = [x, gamma, w_in, wc_pad, w_out]
    if state is not None:
        tm = _row_tile(m, tm)
        assert tm % seq_len == 0 and seq_len >= CONV_WIDTH - 1
        zeros = jnp.zeros((n_seq, seq_len, d), F32)
        p1 = zeros.at[:, 0].set(state[:, 1]).reshape(m, d)
        p2 = zeros.at[:, 0].set(state[:, 0]).at[:, 1].set(state[:, 1]).reshape(m, d)
        in_specs += [pl.BlockSpec((tm, d), row), pl.BlockSpec((tm, d), row)]
        args += [p1, p2]
        u_rows, u_blk, scratch = m, tm, []
        kern = functools.partial(_conv_kernel, tiles_per_seq=0, seq_in_tile=seq_len)
    else:
        tm = _row_tile(seq_len, tm)
        u_rows, u_blk = (m // tm) * SUBLANES, SUBLANES
        scratch = [pltpu.VMEM((SUBLANES, d), F32)]
        kern = functools.partial(_conv_kernel, tiles_per_seq=seq_len // tm, seq_in_tile=0)
    out, u = pl.pallas_call(
        kern,
        grid=(m // tm,),
        in_specs=in_specs,
        out_specs=[pl.BlockSpec((tm, d), row), pl.BlockSpec((u_blk, d), row)],
        out_shape=[jax.ShapeDtypeStruct((m, d), F32), jax.ShapeDtypeStruct((u_rows, d), F32)],
        scratch_shapes=scratch,
        compiler_params=_params("arbitrary"),
        name="conv_mixer",
    )(*args)
    u_last = u.reshape(n_seq, -1, d)[:, -(CONV_WIDTH - 1):]
    return out, u_last


def _topk_bias(gate, blk, valid, axis):
    n = gate.shape[axis]
    g = jnp.where(valid, gate, NEG_INF)
    sel = jnp.zeros(gate.shape, jnp.bool_)
    for _ in range(MOBA_TOPK):
        mx = jnp.max(g, axis=axis, keepdims=True)
        idx = jnp.min(jnp.where(g == mx, blk, n), axis=axis, keepdims=True)
        pick = blk == idx
        sel = sel | (pick & valid)
        g = jnp.where(pick, NEG_INF, g)
    return jnp.where(sel, 0.0, NEG_INF)


def _moba_prompt_kernel(qt_ref, k_ref, vt_ref, km_ref, o_ref, bias_scr, *, nblk, n_tiles):
    i = pl.program_id(2)
    bs = MOBA_BLOCK
    hd = MOBA_HD
    per_tile = LANES // hd
    heads = [(t, h2) for t in range(n_tiles) for h2 in range(per_tile)]
    feat_row = _iota((LANES, 1), 0) // hd
    blk = _iota((nblk, 1), 0)
    causal = _iota((bs, 1), 0) <= _iota((1, bs), 1)
    valid = blk < i

    def key_block(j):
        start = pl.multiple_of(j * bs, bs)
        return k_ref[pl.ds(start, bs), :], vt_ref[0, :, pl.ds(start, bs)]

    def scores(kj, qbs):
        return [_dot(kj[:, t * LANES:(t + 1) * LANES], qb) for (t, _), qb in zip(heads, qbs)]

    def values(vtj, ps):
        return [_dot(vtj[n * hd:(n + 1) * hd], p.astype(BF16)) for n, p in enumerate(ps)]

    qbs = []
    for n, (t, h2) in enumerate(heads):
        qt = qt_ref[0, t * LANES:(t + 1) * LANES, :]
        km = km_ref[:, t * LANES:(t + 1) * LANES]
        qb = jnp.where(feat_row == h2, qt * (hd ** -0.5 * LOG2_E), 0.0).astype(BF16)
        qbs.append(qb)
        bias_scr[n] = _topk_bias(_dot(km.astype(BF16), qb), blk, valid, 0)

    k_own, vt_own = key_block(i)
    ss = [jnp.where(causal, s, NEG_INF) for s in scores(k_own, qbs)]
    ms = [jnp.max(s, axis=0, keepdims=True) for s in ss]
    ps = [jnp.exp2(s - m) for s, m in zip(ss, ms)]
    ls = [jnp.sum(p, axis=0, keepdims=True) for p in ps]
    accs = values(vt_own, ps)

    def body(jj, carry):
        ms, ls, accs = carry
        blocks = [key_block(2 * jj + u) for u in range(2)]
        ss = [scores(blocks[u][0], qbs) for u in range(2)]
        biases = [[bias_scr[n, pl.ds(2 * jj + u, 1), :] for n in range(len(heads))] for u in range(2)]
        tops = [[jnp.max(s, axis=0, keepdims=True) + bias for s, bias in zip(ss[u], biases[u])] for u in range(2)]
        m_new = [jnp.maximum(m, jnp.maximum(t0, t1)) for m, t0, t1 in zip(ms, tops[0], tops[1])]
        alphas = [jnp.exp2(m - mn) for m, mn in zip(ms, m_new)]
        ps = [[jnp.exp2(s - (mn - bias)) for s, mn, bias in zip(ss[u], m_new, biases[u])] for u in range(2)]
        ls = [a * l + jnp.sum(p0, axis=0, keepdims=True) + jnp.sum(p1, axis=0, keepdims=True)
              for a, l, p0, p1 in zip(alphas, ls, ps[0], ps[1])]
        pvs = [values(blocks[u][1], ps[u]) for u in range(2)]
        accs = [a * acc + pv0 + pv1 for a, acc, pv0, pv1 in zip(alphas, accs, pvs[0], pvs[1])]
        return m_new, ls, accs

    ms, ls, accs = lax.fori_loop(0, (i + 1) // 2, body, (ms, ls, accs))
    ot = jnp.concatenate([acc / l for acc, l in zip(accs, ls)], axis=0)
    o_ref[...] = ot.T.astype(o_ref.dtype)


def _moba_prompt(qt, k, vt, kmean):
    batch, d, s_len = qt.shape
    nblk = s_len // MOBA_BLOCK
    assert s_len % MOBA_BLOCK == 0 and nblk % SUBLANES == 0
    n_tiles = MOBA_TILES_PER_STEP
    width = n_tiles * LANES
    km2 = kmean.reshape(batch * nblk, d)
    return pl.pallas_call(
        functools.partial(_moba_prompt_kernel, nblk=nblk, n_tiles=n_tiles),
        grid=(batch, d // width, nblk),
        in_specs=[pl.BlockSpec((1, width, MOBA_BLOCK), lambda b, h, i: (b, h, i)),
                  pl.BlockSpec((s_len, width), lambda b, h, i: (b, h)),
                  pl.BlockSpec((1, width, s_len), lambda b, h, i: (b, h, 0)),
                  pl.BlockSpec((nblk, width), lambda b, h, i: (b, h))],
        out_specs=pl.BlockSpec((MOBA_BLOCK, width), lambda b, h, i: (b * nblk + i, h)),
        out_shape=jax.ShapeDtypeStruct((batch * s_len, d), BF16),
        scratch_shapes=[pltpu.VMEM((width // MOBA_HD, nblk, MOBA_BLOCK), F32)],
        compiler_params=_params("arbitrary", "arbitrary", "arbitrary"),
        name="moba_prompt",
    )(qt, k, vt, km2)


def _moba_paged_kernel(pt_ref, q_ref, kn_ref, vn_ref, *refs, n_pages, page, t_new):
    k_refs = refs[0:n_pages]
    v_refs = refs[n_pages:2 * n_pages]
    o_ref = refs[2 * n_pages]
    qbd_scr, s_scr, bias_scr = refs[2 * n_pages + 1:]
    del pt_ref
    ph = pl.program_id(1)
    ppb = MOBA_BLOCK // page
    n_blk = n_pages // ppb
    n_col = MOBA_HEADS * t_new
    d = D_MODEL
    blk_lane = _iota((1, LANES), 1)
    own_ok = _iota((n_col, t_new), 1) <= _iota((n_col, t_new), 0) % t_new

    def own_scores(qbd):
        return jnp.where(own_ok, _dot_nt(qbd, kn_ref[0].astype(BF16)), NEG_INF)

    def block_bias(bias, p):
        b = p // ppb
        return bias[:, b:b + 1]

    @pl.when(ph == 0)
    def _():
        q = q_ref[0]
        qt = jnp.concatenate([q] * MOBA_HEADS, axis=0)
        bd = _iota((n_col, d), 0) // t_new == _iota((n_col, d), 1) // MOBA_HD
        qbd = jnp.where(bd, qt * (MOBA_HD ** -0.5), 0.0).astype(BF16)
        qbd_scr[...] = qbd
        gate = jnp.zeros((n_col, LANES), F32)
        for b in range(n_blk):
            ssum = None
            for r in range(ppb):
                sc = _dot(qbd, k_refs[b * ppb + r][0].astype(BF16))
                s_scr[b * ppb + r] = sc
                ssum = sc if ssum is None else ssum + sc
            gate = jnp.where(blk_lane == b, jnp.sum(ssum, axis=1, keepdims=True), gate)
        bias = _topk_bias(gate, blk_lane, blk_lane < n_blk, 1)
        bias_scr[...] = bias

    @pl.when(ph == 1)
    def _():
        bias = bias_scr[...]
        s_own = own_scores(qbd_scr[...])
        top = None
        for p in range(n_pages):
            sb = s_scr[p] + block_bias(bias, p)
            top = sb if top is None else jnp.maximum(top, sb)
        m = jnp.maximum(jnp.max(s_own, axis=1, keepdims=True), jnp.max(top, axis=1, keepdims=True))
        e_own = jnp.exp(s_own - m)
        esum = None
        for p in range(n_pages):
            e = jnp.exp(s_scr[p] + block_bias(bias, p) - m)
            s_scr[p] = e
            esum = e if esum is None else esum + e
        inv = 1.0 / (jnp.sum(e_own, axis=1, keepdims=True) + jnp.sum(esum, axis=1, keepdims=True))
        acc = _dot((e_own * inv).astype(BF16), vn_ref[0].astype(BF16))
        for p in range(n_pages):
            acc = acc + _dot_nt((s_scr[p] * inv).astype(BF16), v_refs[p][0].astype(BF16))
        bd = _iota((n_col, d), 0) // t_new == _iota((n_col, d), 1) // MOBA_HD
        a = jnp.where(bd, acc, 0.0)
        out = a[0:t_new]
        for hd in range(1, MOBA_HEADS):
            out = out + a[hd * t_new:(hd + 1) * t_new]
        o_ref[0] = out


def _moba_paged(q, k_new, v_new, cache_kt, cache_vt, page_table):
    db, t_new, d = q.shape
    n_pages = page_table.shape[1]
    page = cache_kt.shape[2]
    past = n_pages * page
    assert past % MOBA_BLOCK == 0 and MOBA_BLOCK % page == 0
    assert t_new <= MOBA_BLOCK and MOBA_TOPK <= past // MOBA_BLOCK <= LANES
    n_col = MOBA_HEADS * t_new
    pt = page_table.reshape(-1).astype(jnp.int32)

    def k_map(r):
        return lambda b, ph, pt_ref: (pt_ref[b * n_pages + r], 0, 0)

    def v_map(r):
        return lambda b, ph, pt_ref: (pt_ref[jnp.where(ph == 0, jnp.maximum(b - 1, 0), b) * n_pages + r], 0, 0)

    tok = pl.BlockSpec((1, t_new, d), lambda b, ph, pt_ref: (b, 0, 0))
    in_specs = [tok, tok, tok]
    in_specs += [pl.BlockSpec((1, d, page), k_map(r)) for r in range(n_pages)]
    in_specs += [pl.BlockSpec((1, d, page), v_map(r)) for r in range(n_pages)]
    grid_spec = pltpu.PrefetchScalarGridSpec(
        num_scalar_prefetch=1,
        grid=(db, 2),
        in_specs=in_specs,
        out_specs=pl.BlockSpec((1, t_new, d), lambda b, ph, pt_ref: (b, 0, 0)),
        scratch_shapes=[pltpu.VMEM((n_col, d), BF16), pltpu.VMEM((n_pages, n_col, page), F32),
                        pltpu.VMEM((n_col, LANES), F32)],
    )
    return pl.pallas_call(
        functools.partial(_moba_paged_kernel, n_pages=n_pages, page=page, t_new=t_new),
        grid_spec=grid_spec,
        out_shape=jax.ShapeDtypeStruct((db, t_new, d), F32),
        compiler_params=_params("arbitrary", "arbitrary"),
        name="moba_paged",
    )(pt, q, k_new, v_new, *([cache_kt] * n_pages), *([cache_vt] * n_pages))


def _feature_major_pages(cache):
    n_phys, page, n_h, hd = cache.shape
    return jnp.transpose(cache, (0, 2, 3, 1)).reshape(n_phys, n_h * hd, page)


def _token_rows(xt, batch):
    _, _, s_len = xt.shape
    return jnp.transpose(xt.reshape(batch, MOBA_HEADS, MOBA_HD, s_len), (0, 3, 1, 2))


def kernel(x_prompt, x_sample, state_ret, cache_k, cache_v, state_conv, page_table, norm_mix, norm_ff, norm_final, ret_w_qkvg, ret_w_o, moba_w_qkv, moba_w_o, conv_w_in, conv_w, conv_w_out, mlp_w_up, mlp_w_down):
    b, s, d = x_prompt.shape
    db, t, _ = x_sample.shape
    depth = norm_mix.shape[0]
    n_pages = page_table.shape[1]
    page = cache_k.shape[2]
    past_len = n_pages * page

    xp = x_prompt.reshape(b * s, d)
    xs = x_sample.reshape(db * t, d)
    tm_p = TM_PROMPT
    tm_s = db * t

    cos_p, sin_p = _rope_tables(jnp.arange(s, dtype=jnp.int32))
    cos_s, sin_s = _rope_tables(past_len + jnp.arange(t, dtype=jnp.int32))
    cos_s = jnp.tile(cos_s, (tm_s // t, 1))
    sin_s = jnp.tile(sin_s, (tm_s // t, 1))
    tabs_p = _ret_tables(RET_CHUNK)
    tabs_s = _ret_tables(t)

    ret_w = ret_w_qkvg.astype(BF16)
    ret_wo = ret_w_o.astype(BF16)
    moba_wo = moba_w_o.astype(BF16)
    up_w = mlp_w_up.astype(BF16)
    down_w = mlp_w_down.astype(BF16)

    ret_p, ret_s_all, kp_l, vp_l, ks_l, vs_l, conv_p, conv_s = [], None, [], [], [], [], [], []
    for i in range(depth):
        kind = i % N_MIXERS
        j = i // N_MIXERS
        g_mix = norm_mix[i].reshape(1, d)
        g_ff = norm_ff[i].reshape(1, d)
        if kind == 0:
            wo_all = ret_wo
            q, k, v, g = _ret_proj(xp, g_mix, ret_w, j, cos_p, sin_p, tm=tm_p, qv_dtype=BF16)
            op, st_p = _ret_prompt(q, k, v, g, tabs_p, batch=b, chunk=RET_CHUNK)
            q, k, v, g = _ret_proj(xs, g_mix, ret_w, j, cos_s, sin_s, tm=tm_s, qv_dtype=F32)
            os_, ret_s_all = _ret_core(q, k, v, g, tabs_s, batch=db, chunk=t, nb=RET_SEQS_PER_STEP,
                                       state_all=state_ret, layer=j, out_all=ret_s_all)
            ret_p.append(st_p)
        elif kind == 1:
            wo_all = moba_wo
            qt, k, kt, vt, vtb, kmean = _moba_proj_t(xp, g_mix, moba_w_qkv[j], batch=b, tm=tm_p)
            op = _moba_prompt(qt, k, vtb, kmean)
            kp_l.append(_token_rows(kt, b))
            vp_l.append(_token_rows(vt, b))
            q, k, v = _norm_proj(xs, g_mix, moba_w_qkv[j].astype(BF16), (d, d, d), (F32, F32, F32), tm=tm_s)
            os_ = _moba_paged(q.reshape(db, t, d), k.reshape(db, t, d), v.reshape(db, t, d),
                              _feature_major_pages(cache_k[j]), _feature_major_pages(cache_v[j]),
                              page_table).reshape(db * t, d)
            ks_l.append(k.reshape(db, t, MOBA_HEADS, MOBA_HD))
            vs_l.append(v.reshape(db, t, MOBA_HEADS, MOBA_HD))
        else:
            w_in = conv_w_in[j].astype(BF16)
            w_out = conv_w_out[j].astype(BF16)
            xp, buf_p = _conv_mixer(xp, g_mix, w_in, conv_w[j], w_out, tm=tm_p, seq_len=s)
            xs, buf_s = _conv_mixer(xs, g_mix, w_in, conv_w[j], w_out, tm=tm_s, seq_len=t,
                                    state=state_conv[j])
            conv_p.append(buf_p)
            conv_s.append(buf_s)
            op = os_ = wo_all = None
        g_fin = norm_final.reshape(1, d) if i == depth - 1 else None
        xp = _mlp(xp, g_ff, up_w, down_w, i, tm=TM_MLP, tf=TF_MLP, a=op, w_o_all=wo_all, w_o_layer=j,
                  final_gamma=g_fin)
        xs = _mlp(xs, g_ff, up_w, down_w, i, tm=tm_s, tf=TF_MLP, a=os_, w_o_all=wo_all, w_o_layer=j,
                  final_gamma=g_fin)
    y_prompt = xp.reshape(b, s, d)
    y_sample = xs.reshape(db, t, d)
    return (y_prompt, y_sample, jnp.stack(ret_p), ret_s_all, jnp.stack(kp_l), jnp.stack(vp_l),
            jnp.stack(ks_l), jnp.stack(vs_l), jnp.stack(conv_p), jnp.stack(conv_s))
```
